```python
import math
import jax, jax.numpy as jnp
from jax import lax
import numpy as np

D_MODEL = 2048
BATCH = 8
SEQ = 8192
DEPTH = 4

EPS = 1e-6
N_BRANCH = 3
BRANCH_WIDTH = 1024

MLA_HEADS = 8
QK_NOPE = 128
QK_ROPE = 64
V_HEAD = 128
Q_LORA = 512
KV_LORA = 512
ROPE_THETA = 10000.0
Q_BLOCK = 128

LRU_WIDTH = 1024
LRU_BLOCKS = 8
LRU_BLOCK_W = LRU_WIDTH // LRU_BLOCKS
LRU_C = 8.0
CONV_W = 4

GDN_HEADS = 8
GDN_DK = 128
GDN_DV = 128
GDN_CHUNK = 64

FFN_HIDDEN = -(-(8 * D_MODEL) // (3 * 256)) * 256

IN_SIZES = [Q_LORA, KV_LORA, QK_ROPE,
            LRU_WIDTH, LRU_WIDTH,
            GDN_HEADS * GDN_DK, GDN_HEADS * GDN_DK, GDN_HEADS * GDN_DV, GDN_HEADS * GDN_DV, GDN_HEADS, GDN_HEADS,
            N_BRANCH * D_MODEL]
IN_COLS = sum(IN_SIZES)
IN_SPLITS = [sum(IN_SIZES[:i]) for i in range(1, len(IN_SIZES))]

kernel_name = 'hybrid_mla_rglru_gdn_gated_merge'


def rmsnorm(x, w):
    xf = x.astype(jnp.float32)
    y = xf * lax.rsqrt(jnp.mean(xf * xf, axis=-1, keepdims=True) + EPS)
    return (y * w.astype(jnp.float32)).astype(x.dtype)


def causal_depthwise_conv(x, w):
    c = x.shape[-1]
    return lax.conv_general_dilated(
        x, w.astype(x.dtype)[:, None, :], window_strides=(1,), padding=((CONV_W - 1, 0),),
        dimension_numbers=('NWC', 'WIO', 'NWC'), feature_group_count=c)


def apply_rope(x, positions):
    half = QK_ROPE // 2
    inv_freq = ROPE_THETA ** (-jnp.arange(half, dtype=jnp.float32) / half)
    ang = positions.astype(jnp.float32)[:, :, None, None] * inv_freq
    cos, sin = jnp.cos(ang), jnp.sin(ang)
    xf = x.astype(jnp.float32)
    x1, x2 = xf[..., :half], xf[..., half:]
    return jnp.concatenate([x1 * cos - x2 * sin, x2 * cos + x1 * sin], axis=-1).astype(x.dtype)


def block_causal_attention(q, k, v):
    b, s, h, dqk = q.shape
    dv = v.shape[-1]
    nblk = s // Q_BLOCK
    scale = dqk ** -0.5
    qb = jnp.moveaxis(q.reshape(b, nblk, Q_BLOCK, h, dqk), 1, 0)
    starts = jnp.arange(nblk, dtype=jnp.int32) * Q_BLOCK
    kpos = jnp.arange(s, dtype=jnp.int32)

    def one_block(args):
        q_blk, start = args
        sc = jnp.einsum('bqhd,bkhd->bhqk', q_blk, k, preferred_element_type=jnp.float32) * scale
        qpos = start + jnp.arange(Q_BLOCK, dtype=jnp.int32)
        sc = jnp.where(kpos[None, :] <= qpos[:, None], sc, -jnp.inf)
        p = jax.nn.softmax(sc, axis=-1).astype(v.dtype)
        return jnp.einsum('bhqk,bkhd->bqhd', p, v)

    out = lax.map(one_block, (qb, starts))
    return jnp.moveaxis(out, 0, 1).reshape(b, s, h, dv)


def mla_branch(c_q, c_kv, k_rope, positions, q_norm, w_uq, kv_norm, w_ukv):
    b, s, _ = c_q.shape
    q = (rmsnorm(c_q, q_norm) @ w_uq).reshape(b, s, MLA_HEADS, QK_NOPE + QK_ROPE)
    q = jnp.concatenate([q[..., :QK_NOPE], apply_rope(q[..., QK_NOPE:], positions)], axis=-1)
    kv = (rmsnorm(c_kv, kv_norm) @ w_ukv).reshape(b, s, MLA_HEADS, QK_NOPE + V_HEAD)
    k_nope, v = kv[..., :QK_NOPE], kv[..., QK_NOPE:]
    k_pe = apply_rope(k_rope[:, :, None, :], positions)
    k = jnp.concatenate([k_nope, jnp.broadcast_to(k_pe, (b, s, MLA_HEADS, QK_ROPE))], axis=-1)
    o = block_causal_attention(q, k, v)
    return o.reshape(b, s, MLA_HEADS * V_HEAD)


def _lru_combine(e1, e2):
    a1, b1 = e1
    a2, b2 = e2
    return a1 * a2, a2 * b1 + b2


def rglru_branch(x_in, y_in, conv_w, conv_b, w_a, b_a, w_x, b_x, lam):
    b, s, _ = x_in.shape
    xc = causal_depthwise_conv(x_in, conv_w) + conv_b
    xb = xc.reshape(b, s, LRU_BLOCKS, LRU_BLOCK_W)
    r = jax.nn.sigmoid(jnp.einsum('bsgi,gij->bsgj', xb, w_a).reshape(b, s, LRU_WIDTH) + b_a)
    gi = jax.nn.sigmoid(jnp.einsum('bsgi,gij->bsgj', xb, w_x).reshape(b, s, LRU_WIDTH) + b_x)
    log_a = -LRU_C * r.astype(jnp.float32) * jax.nn.softplus(-lam.astype(jnp.float32))
    a = jnp.exp(log_a)
    mult = jnp.sqrt(-jnp.expm1(2.0 * log_a))
    bx = mult * (gi * xc).astype(jnp.float32)
    _, h = lax.associative_scan(_lru_combine, (a, bx), axis=1)
    return h.astype(x_in.dtype) * jax.nn.gelu(y_in)


def _l2norm(t):
    return t * lax.rsqrt(jnp.sum(t * t, axis=-1, keepdims=True) + EPS)


def chunk_gated_delta_rule(q, k, v, g, beta):
    b, s, h, dk = q.shape
    dv = v.shape[-1]
    n = s // GDN_CHUNK

    def chunks(t):
        t = t.reshape((b, n, GDN_CHUNK, h) + t.shape[3:])
        return jnp.moveaxis(t, 3, 1)

    q = chunks(q) * dk ** -0.5
    k, v, g, beta = chunks(k), chunks(v), chunks(g), chunks(beta)
    gc = jnp.cumsum(g, axis=-1)
    idx = jnp.arange(GDN_CHUNK)
    incl = idx[:, None] >= idx[None, :]
    strict = idx[:, None] > idx[None, :]
    decay = jnp.exp(jnp.where(incl, gc[..., :, None] - gc[..., None, :], -jnp.inf))
    k_beta = k * beta[..., None]
    v_beta = v * beta[..., None]
    kkt = jnp.einsum('bhnid,bhnjd->bhnij', k_beta, k) * decay
    a_mat = jnp.where(strict, kkt, 0.0) + jnp.eye(GDN_CHUNK, dtype=q.dtype)
    u = lax.linalg.triangular_solve(a_mat, v_beta, left_side=True, lower=True, unit_diagonal=True)
    w = lax.linalg.triangular_solve(a_mat, k_beta * jnp.exp(gc)[..., None], left_side=True, lower=True, unit_diagonal=True)
    qk = jnp.where(incl, jnp.einsum('bhnid,bhnjd->bhnij', q, k) * decay, 0.0)
    q_dec = q * jnp.exp(gc)[..., None]
    k_tail = k * jnp.exp(gc[..., -1:] - gc)[..., None]
    g_tail = jnp.exp(gc[..., -1])
    xs = (jnp.moveaxis(u, 2, 0), jnp.moveaxis(w, 2, 0), jnp.moveaxis(qk, 2, 0),
          jnp.moveaxis(q_dec, 2, 0), jnp.moveaxis(k_tail, 2, 0), jnp.moveaxis(g_tail, 2, 0))

    def step(state, inp):
        u_n, w_n, qk_n, qd_n, kt_n, gt_n = inp
        v_new = u_n - jnp.einsum('bhid,bhde->bhie', w_n, state)
        o_n = jnp.einsum('bhid,bhde->bhie', qd_n, state) + jnp.einsum('bhij,bhje->bhie', qk_n, v_new)
        state = state * gt_n[..., None, None] + jnp.einsum('bhid,bhie->bhde', kt_n, v_new)
        return state, o_n

    state0 = jnp.zeros((b, h, dk, dv), q.dtype)
    _, o = lax.scan(step, state0, xs)
    o = jnp.moveaxis(o, 0, 2)
    return jnp.moveaxis(o, 1, 3).reshape(b, s, h, dv)


def gdn_branch(q, k, v, z, b_logit, a_logit, conv_w, a_log, dt_bias, norm_w):
    b, s, _ = q.shape
    qkv = jax.nn.silu(causal_depthwise_conv(jnp.concatenate([q, k, v], axis=-1), conv_w))
    qc, kc, vc = jnp.split(qkv, [GDN_HEADS * GDN_DK, 2 * GDN_HEADS * GDN_DK], axis=-1)
    qc = _l2norm(qc.reshape(b, s, GDN_HEADS, GDN_DK).astype(jnp.float32))
    kc = _l2norm(kc.reshape(b, s, GDN_HEADS, GDN_DK).astype(jnp.float32))
    vc = vc.reshape(b, s, GDN_HEADS, GDN_DV).astype(jnp.float32)
    beta = jax.nn.sigmoid(b_logit.astype(jnp.float32))
    g = -jnp.exp(a_log.astype(jnp.float32)) * jax.nn.softplus(a_logit.astype(jnp.float32) + dt_bias.astype(jnp.float32))
    o = chunk_gated_delta_rule(qc, kc, vc, g, beta)
    o = rmsnorm(o, norm_w) * jax.nn.silu(z.reshape(b, s, GDN_HEADS, GDN_DV).astype(jnp.float32))
    return o.reshape(b, s, GDN_HEADS * GDN_DV).astype(q.dtype)


def _fwd_setup_inputs(seed: int = 0) -> dict:
    key = jax.random.key(seed)
    ks = jax.random.split(key, 32)
    f32 = jnp.float32

    def nrm(k, shape, fan_in):
        return jax.random.normal(k, shape, f32) * fan_in ** -0.5

    def gain(k, shape):
        return 1.0 + 0.02 * jax.random.normal(k, shape, f32)

    def small(k, shape):
        return 0.01 * jax.random.normal(k, shape, f32)

    x = jax.random.normal(ks[0], (BATCH, SEQ, D_MODEL), f32)
    positions = jnp.broadcast_to(jnp.arange(SEQ, dtype=jnp.int32), (BATCH, SEQ))
    u = jax.random.uniform(ks[13], (DEPTH, LRU_WIDTH), f32, 0.9, 0.999)
    a0 = u ** (1.0 / LRU_C)
    lru_lambda = jnp.log(a0) - jnp.log1p(-a0)
    gdn_a_log = jnp.log(jax.random.uniform(ks[15], (DEPTH, GDN_HEADS), f32, 1.0, 16.0))
    dt = jnp.exp(jax.random.uniform(ks[16], (DEPTH, GDN_HEADS), f32, math.log(1e-3), math.log(1e-1)))
    gdn_dt_bias = dt + jnp.log(-jnp.expm1(-dt))
    return {
        'x': x,
        'positions': positions,
        'norm_mix': gain(ks[1], (DEPTH, D_MODEL)),
        'w_in': nrm(ks[2], (DEPTH, D_MODEL, IN_COLS), D_MODEL),
        'mla_q_norm': gain(ks[3], (DEPTH, Q_LORA)),
        'mla_w_uq': nrm(ks[4], (DEPTH, Q_LORA, MLA_HEADS * (QK_NOPE + QK_ROPE)), Q_LORA),
        'mla_kv_norm': gain(ks[5], (DEPTH, KV_LORA)),
        'mla_w_ukv': nrm(ks[6], (DEPTH, KV_LORA, MLA_HEADS * (QK_NOPE + V_HEAD)), KV_LORA),
        'lru_conv_w': nrm(ks[7], (DEPTH, CONV_W, LRU_WIDTH), CONV_W),
        'lru_conv_b': small(ks[8], (DEPTH, LRU_WIDTH)),
        'lru_w_a': nrm(ks[9], (DEPTH, LRU_BLOCKS, LRU_BLOCK_W, LRU_BLOCK_W), LRU_BLOCK_W),
        'lru_b_a': small(ks[10], (DEPTH, LRU_WIDTH)),
        'lru_w_x': nrm(ks[11], (DEPTH, LRU_BLOCKS, LRU_BLOCK_W, LRU_BLOCK_W), LRU_BLOCK_W),
        'lru_b_x': small(ks[12], (DEPTH, LRU_WIDTH)),
        'lru_lambda': lru_lambda,
        'gdn_conv_w': nrm(ks[14], (DEPTH, CONV_W, 2 * GDN_HEADS * GDN_DK + GDN_HEADS * GDN_DV), CONV_W),
        'gdn_a_log': gdn_a_log,
        'gdn_dt_bias': gdn_dt_bias,
        'gdn_norm': gain(ks[17], (DEPTH, GDN_DV)),
        'w_branch': nrm(ks[18], (DEPTH, N_BRANCH, BRANCH_WIDTH, D_MODEL), BRANCH_WIDTH),
        'b_gate': small(ks[19], (DEPTH, N_BRANCH, D_MODEL)),
        'w_out': nrm(ks[20], (DEPTH, D_MODEL, D_MODEL), D_MODEL),
        'norm_ffn': gain(ks[21], (DEPTH, D_MODEL)),
        'ffn_w_gate': nrm(ks[22], (DEPTH, D_MODEL, FFN_HIDDEN), D_MODEL),
        'ffn_w_up': nrm(ks[23], (DEPTH, D_MODEL, FFN_HIDDEN), D_MODEL),
        'ffn_w_down': nrm(ks[24], (DEPTH, FFN_HIDDEN, D_MODEL), FFN_HIDDEN),
        'norm_final': gain(ks[25], (D_MODEL,)),
    }


def _fwd_reference(x, positions, norm_mix, w_in, mla_q_norm, mla_w_uq, mla_kv_norm, mla_w_ukv,
              lru_conv_w, lru_conv_b, lru_w_a, lru_b_a, lru_w_x, lru_b_x, lru_lambda,
              gdn_conv_w, gdn_a_log, gdn_dt_bias, gdn_norm, w_branch, b_gate, w_out,
              norm_ffn, ffn_w_gate, ffn_w_up, ffn_w_down, norm_final):
    b, s, _ = x.shape
    for l in range(DEPTH):
        h = rmsnorm(x, norm_mix[l])
        proj = h @ w_in[l]
        (c_q, c_kv, k_rope, lru_x, lru_y, g_q, g_k, g_v, g_z, g_b, g_a, gate_logits) = jnp.split(proj, IN_SPLITS, axis=-1)
        y_mla = mla_branch(c_q, c_kv, k_rope, positions, mla_q_norm[l], mla_w_uq[l], mla_kv_norm[l], mla_w_ukv[l])
        y_lru = rglru_branch(lru_x, lru_y, lru_conv_w[l], lru_conv_b[l], lru_w_a[l], lru_b_a[l],
                             lru_w_x[l], lru_b_x[l], lru_lambda[l])
        y_gdn = gdn_branch(g_q, g_k, g_v, g_z, g_b, g_a, gdn_conv_w[l], gdn_a_log[l], gdn_dt_bias[l], gdn_norm[l])
        branches = jnp.stack([y_mla, y_lru, y_gdn], axis=2)
        up = jnp.einsum('bsnw,nwd->bsnd', branches, w_branch[l])
        gates = jax.nn.sigmoid(gate_logits.reshape(b, s, N_BRANCH, D_MODEL) + b_gate[l])
        mixed = jnp.sum(gates * up, axis=2)
        x = x + mixed @ w_out[l]
        h = rmsnorm(x, norm_ffn[l])
        x = x + (jax.nn.silu(h @ ffn_w_gate[l]) * (h @ ffn_w_up[l])) @ ffn_w_down[l]
    return rmsnorm(x, norm_final)


import jax as _jax
import jax.numpy as _jnp

TWIN_FORMAT = 'train_step'
FWD_PARAMS = ['x', 'positions', 'norm_mix', 'w_in', 'mla_q_norm', 'mla_w_uq', 'mla_kv_norm', 'mla_w_ukv', 'lru_conv_w', 'lru_conv_b', 'lru_w_a', 'lru_b_a', 'lru_w_x', 'lru_b_x', 'lru_lambda', 'gdn_conv_w', 'gdn_a_log', 'gdn_dt_bias', 'gdn_norm', 'w_branch', 'b_gate', 'w_out', 'norm_ffn', 'ffn_w_gate', 'ffn_w_up', 'ffn_w_down', 'norm_final']
TWIN_WEIGHTS = ['norm_mix', 'w_in', 'mla_q_norm', 'mla_w_uq', 'mla_kv_norm', 'mla_w_ukv', 'lru_conv_w', 'lru_conv_b', 'lru_w_a', 'lru_b_a', 'lru_w_x', 'lru_b_x', 'lru_lambda', 'gdn_conv_w', 'gdn_a_log', 'gdn_dt_bias', 'gdn_norm', 'w_branch', 'b_gate', 'w_out', 'norm_ffn', 'ffn_w_gate', 'ffn_w_up', 'ffn_w_down', 'norm_final']
TWIN_DIFF_INPUT = 'x'
TWIN_INPUTS = ['x', 'positions', 'norm_mix', 'w_in', 'mla_q_norm', 'mla_w_uq', 'mla_kv_norm', 'mla_w_ukv', 'lru_conv_w', 'lru_conv_b', 'lru_w_a', 'lru_b_a', 'lru_w_x', 'lru_b_x', 'lru_lambda', 'gdn_conv_w', 'gdn_a_log', 'gdn_dt_bias', 'gdn_norm', 'w_branch', 'b_gate', 'w_out', 'norm_ffn', 'ffn_w_gate', 'ffn_w_up', 'ffn_w_down', 'norm_final', 'loss_target', 'm_norm_mix', 'm_w_in', 'm_mla_q_norm', 'm_mla_w_uq', 'm_mla_kv_norm', 'm_mla_w_ukv', 'm_lru_conv_w', 'm_lru_conv_b', 'm_lru_w_a', 'm_lru_b_a', 'm_lru_w_x', 'm_lru_b_x', 'm_lru_lambda', 'm_gdn_conv_w', 'm_gdn_a_log', 'm_gdn_dt_bias', 'm_gdn_norm', 'm_w_branch', 'm_b_gate', 'm_w_out', 'm_norm_ffn', 'm_ffn_w_gate', 'm_ffn_w_up', 'm_ffn_w_down', 'm_norm_final', 'v_norm_mix', 'v_w_in', 'v_mla_q_norm', 'v_mla_w_uq', 'v_mla_kv_norm', 'v_mla_w_ukv', 'v_lru_conv_w', 'v_lru_conv_b', 'v_lru_w_a', 'v_lru_b_a', 'v_lru_w_x', 'v_lru_b_x', 'v_lru_lambda', 'v_gdn_conv_w', 'v_gdn_a_log', 'v_gdn_dt_bias', 'v_gdn_norm', 'v_w_branch', 'v_b_gate', 'v_w_out', 'v_norm_ffn', 'v_ffn_w_gate', 'v_ffn_w_up', 'v_ffn_w_down', 'v_norm_final']
TWIN_OUTPUTS = ['loss', 'grad_x', 'grad_norm_mix', 'grad_w_in', 'grad_mla_q_norm', 'grad_mla_w_uq', 'grad_mla_kv_norm', 'grad_mla_w_ukv', 'grad_lru_conv_w', 'grad_lru_conv_b', 'grad_lru_w_a', 'grad_lru_b_a', 'grad_lru_w_x', 'grad_lru_b_x', 'grad_lru_lambda', 'grad_gdn_conv_w', 'grad_gdn_a_log', 'grad_gdn_dt_bias', 'grad_gdn_norm', 'grad_w_branch', 'grad_b_gate', 'grad_w_out', 'grad_norm_ffn', 'grad_ffn_w_gate', 'grad_ffn_w_up', 'grad_ffn_w_down', 'grad_norm_final', 'delta_norm_mix', 'delta_w_in', 'delta_mla_q_norm', 'delta_mla_w_uq', 'delta_mla_kv_norm', 'delta_mla_w_ukv', 'delta_lru_conv_w', 'delta_lru_conv_b', 'delta_lru_w_a', 'delta_lru_b_a', 'delta_lru_w_x', 'delta_lru_b_x', 'delta_lru_lambda', 'delta_gdn_conv_w', 'delta_gdn_a_log', 'delta_gdn_dt_bias', 'delta_gdn_norm', 'delta_w_branch', 'delta_b_gate', 'delta_w_out', 'delta_norm_ffn', 'delta_ffn_w_gate', 'delta_ffn_w_up', 'delta_ffn_w_down', 'delta_norm_final', 'new_m_norm_mix', 'new_m_w_in', 'new_m_mla_q_norm', 'new_m_mla_w_uq', 'new_m_mla_kv_norm', 'new_m_mla_w_ukv', 'new_m_lru_conv_w', 'new_m_lru_conv_b', 'new_m_lru_w_a', 'new_m_lru_b_a', 'new_m_lru_w_x', 'new_m_lru_b_x', 'new_m_lru_lambda', 'new_m_gdn_conv_w', 'new_m_gdn_a_log', 'new_m_gdn_dt_bias', 'new_m_gdn_norm', 'new_m_w_branch', 'new_m_b_gate', 'new_m_w_out', 'new_m_norm_ffn', 'new_m_ffn_w_gate', 'new_m_ffn_w_up', 'new_m_ffn_w_down', 'new_m_norm_final', 'new_v_norm_mix', 'new_v_w_in', 'new_v_mla_q_norm', 'new_v_mla_w_uq', 'new_v_mla_kv_norm', 'new_v_mla_w_ukv', 'new_v_lru_conv_w', 'new_v_lru_conv_b', 'new_v_lru_w_a', 'new_v_lru_b_a', 'new_v_lru_w_x', 'new_v_lru_b_x', 'new_v_lru_lambda', 'new_v_gdn_conv_w', 'new_v_gdn_a_log', 'new_v_gdn_dt_bias', 'new_v_gdn_norm', 'new_v_w_branch', 'new_v_b_gate', 'new_v_w_out', 'new_v_norm_ffn', 'new_v_ffn_w_gate', 'new_v_ffn_w_up', 'new_v_ffn_w_down', 'new_v_norm_final']
TWIN_LEAF_KINDS = {'loss': 'loss', 'grad_x': 'grad_x', 'grad_norm_mix': 'grad_w', 'grad_w_in': 'grad_w', 'grad_mla_q_norm': 'grad_w', 'grad_mla_w_uq': 'grad_w', 'grad_mla_kv_norm': 'grad_w', 'grad_mla_w_ukv': 'grad_w', 'grad_lru_conv_w': 'grad_w', 'grad_lru_conv_b': 'grad_w', 'grad_lru_w_a': 'grad_w', 'grad_lru_b_a': 'grad_w', 'grad_lru_w_x': 'grad_w', 'grad_lru_b_x': 'grad_w', 'grad_lru_lambda': 'grad_w', 'grad_gdn_conv_w': 'grad_w', 'grad_gdn_a_log': 'grad_w', 'grad_gdn_dt_bias': 'grad_w', 'grad_gdn_norm': 'grad_w', 'grad_w_branch': 'grad_w', 'grad_b_gate': 'grad_w', 'grad_w_out': 'grad_w', 'grad_norm_ffn': 'grad_w', 'grad_ffn_w_gate': 'grad_w', 'grad_ffn_w_up': 'grad_w', 'grad_ffn_w_down': 'grad_w', 'grad_norm_final': 'grad_w', 'delta_norm_mix': 'delta_w', 'delta_w_in': 'delta_w', 'delta_mla_q_norm': 'delta_w', 'delta_mla_w_uq': 'delta_w', 'delta_mla_kv_norm': 'delta_w', 'delta_mla_w_ukv': 'delta_w', 'delta_lru_conv_w': 'delta_w', 'delta_lru_conv_b': 'delta_w', 'delta_lru_w_a': 'delta_w', 'delta_lru_b_a': 'delta_w', 'delta_lru_w_x': 'delta_w', 'delta_lru_b_x': 'delta_w', 'delta_lru_lambda': 'delta_w', 'delta_gdn_conv_w': 'delta_w', 'delta_gdn_a_log': 'delta_w', 'delta_gdn_dt_bias': 'delta_w', 'delta_gdn_norm': 'delta_w', 'delta_w_branch': 'delta_w', 'delta_b_gate': 'delta_w', 'delta_w_out': 'delta_w', 'delta_norm_ffn': 'delta_w', 'delta_ffn_w_gate': 'delta_w', 'delta_ffn_w_up': 'delta_w', 'delta_ffn_w_down': 'delta_w', 'delta_norm_final': 'delta_w', 'new_m_norm_mix': 'new_m', 'new_m_w_in': 'new_m', 'new_m_mla_q_norm': 'new_m', 'new_m_mla_w_uq': 'new_m', 'new_m_mla_kv_norm': 'new_m', 'new_m_mla_w_ukv': 'new_m', 'new_m_lru_conv_w': 'new_m', 'new_m_lru_conv_b': 'new_m', 'new_m_lru_w_a': 'new_m', 'new_m_lru_b_a': 'new_m', 'new_m_lru_w_x': 'new_m', 'new_m_lru_b_x': 'new_m', 'new_m_lru_lambda': 'new_m', 'new_m_gdn_conv_w': 'new_m', 'new_m_gdn_a_log': 'new_m', 'new_m_gdn_dt_bias': 'new_m', 'new_m_gdn_norm': 'new_m', 'new_m_w_branch': 'new_m', 'new_m_b_gate': 'new_m', 'new_m_w_out': 'new_m', 'new_m_norm_ffn': 'new_m', 'new_m_ffn_w_gate': 'new_m', 'new_m_ffn_w_up': 'new_m', 'new_m_ffn_w_down': 'new_m', 'new_m_norm_final': 'new_m', 'new_v_norm_mix': 'new_v', 'new_v_w_in': 'new_v', 'new_v_mla_q_norm': 'new_v', 'new_v_mla_w_uq': 'new_v', 'new_v_mla_kv_norm': 'new_v', 'new_v_mla_w_ukv': 'new_v', 'new_v_lru_conv_w': 'new_v', 'new_v_lru_conv_b': 'new_v', 'new_v_lru_w_a': 'new_v', 'new_v_lru_b_a': 'new_v', 'new_v_lru_w_x': 'new_v', 'new_v_lru_b_x': 'new_v', 'new_v_lru_lambda': 'new_v', 'new_v_gdn_conv_w': 'new_v', 'new_v_gdn_a_log': 'new_v', 'new_v_gdn_dt_bias': 'new_v', 'new_v_gdn_norm': 'new_v', 'new_v_w_branch': 'new_v', 'new_v_b_gate': 'new_v', 'new_v_w_out': 'new_v', 'new_v_norm_ffn': 'new_v', 'new_v_ffn_w_gate': 'new_v', 'new_v_ffn_w_up': 'new_v', 'new_v_ffn_w_down': 'new_v', 'new_v_norm_final': 'new_v'}


def _forward(args):
    return _fwd_reference(*[args[k] for k in FWD_PARAMS])


def _output_shape():
    def fwd():
        inp = _fwd_setup_inputs(0)
        return _fwd_reference(*[inp[k] for k in FWD_PARAMS])
    out = _jax.eval_shape(fwd)
    return out.shape, out.dtype

N_MICROBATCH = 1
ADAM_LR = 0.001
ADAM_B1 = 0.9
ADAM_B2 = 0.999
ADAM_EPS = 1e-08
ADAM_WD = 0.01
ADAM_STEP = 10
PER_EXAMPLE_BATCH_AXIS = {'x': 0, 'positions': 0, 'loss_target': 0}
SHARED_INPUTS = []
_WEIGHT_DTYPES = {'norm_mix': _jnp.float32, 'w_in': _jnp.float32, 'mla_q_norm': _jnp.float32, 'mla_w_uq': _jnp.float32, 'mla_kv_norm': _jnp.float32, 'mla_w_ukv': _jnp.float32, 'lru_conv_w': _jnp.float32, 'lru_conv_b': _jnp.float32, 'lru_w_a': _jnp.float32, 'lru_b_a': _jnp.float32, 'lru_w_x': _jnp.float32, 'lru_b_x': _jnp.float32, 'lru_lambda': _jnp.float32, 'gdn_conv_w': _jnp.float32, 'gdn_a_log': _jnp.float32, 'gdn_dt_bias': _jnp.float32, 'gdn_norm': _jnp.float32, 'w_branch': _jnp.float32, 'b_gate': _jnp.float32, 'w_out': _jnp.float32, 'norm_ffn': _jnp.float32, 'ffn_w_gate': _jnp.float32, 'ffn_w_up': _jnp.float32, 'ffn_w_down': _jnp.float32, 'norm_final': _jnp.float32}
MOMENT_SCALE = {'norm_mix': 9.132009e-02, 'w_in': 3.546647e-02, 'mla_q_norm': 2.623797e-02, 'mla_w_uq': 1.507089e-02, 'mla_kv_norm': 4.528259e-02, 'mla_w_ukv': 2.242676e-02, 'lru_conv_w': 6.142359e-02, 'lru_conv_b': 6.517787e-01, 'lru_w_a': 1.652065e-02, 'lru_b_a': 1.531668e-02, 'lru_w_x': 2.952688e-02, 'lru_b_x': 2.256784e-02, 'lru_lambda': 3.032186e-02, 'gdn_conv_w': 4.873982e-02, 'gdn_a_log': 1.835438e-01, 'gdn_dt_bias': 1.777640e-01, 'gdn_norm': 2.131743e-01, 'w_branch': 3.440053e-02, 'b_gate': 1.360728e-02, 'w_out': 5.969182e-02, 'norm_ffn': 9.364456e-02, 'ffn_w_gate': 4.070154e-02, 'ffn_w_up': 3.944251e-02, 'ffn_w_down': 6.539257e-02, 'norm_final': 3.197356e+01}


def _to_microbatches(a, axis):
    t = _jnp.moveaxis(a, axis, 0)
    t = t.reshape((N_MICROBATCH, t.shape[0] // N_MICROBATCH) + t.shape[1:])
    return _jnp.moveaxis(t, 1, axis + 1)


def setup_inputs(seed: int = 0) -> dict:
    inp = _fwd_setup_inputs(seed)
    key = _jax.random.fold_in(_jax.random.key(seed), 7919)
    shape, _ = _output_shape()
    out = dict(inp)
    out["loss_target"] = _jax.random.normal(_jax.random.fold_in(key, 0), shape, _jnp.float32)
    for i, name in enumerate(TWIN_WEIGHTS):
        w = inp[name].astype(_jnp.float32)
        if MOMENT_SCALE is None:
            s = _jnp.sqrt(_jnp.mean(_jnp.square(w)) + 1e-30)
        else:
            s = MOMENT_SCALE[name]
        km, kv = _jax.random.split(_jax.random.fold_in(key, i + 1))
        out[name] = w
        out["m_" + name] = s * _jax.random.normal(km, w.shape, _jnp.float32)
        out["v_" + name] = (s * s) * _jax.random.uniform(kv, w.shape, _jnp.float32, 0.5, 1.5)
    if N_MICROBATCH > 1:
        for name, axis in PER_EXAMPLE_BATCH_AXIS.items():
            out[name] = _to_microbatches(out[name], axis)
    return {'x': out['x'], 'positions': out['positions'], 'norm_mix': out['norm_mix'], 'w_in': out['w_in'], 'mla_q_norm': out['mla_q_norm'], 'mla_w_uq': out['mla_w_uq'], 'mla_kv_norm': out['mla_kv_norm'], 'mla_w_ukv': out['mla_w_ukv'], 'lru_conv_w': out['lru_conv_w'], 'lru_conv_b': out['lru_conv_b'], 'lru_w_a': out['lru_w_a'], 'lru_b_a': out['lru_b_a'], 'lru_w_x': out['lru_w_x'], 'lru_b_x': out['lru_b_x'], 'lru_lambda': out['lru_lambda'], 'gdn_conv_w': out['gdn_conv_w'], 'gdn_a_log': out['gdn_a_log'], 'gdn_dt_bias': out['gdn_dt_bias'], 'gdn_norm': out['gdn_norm'], 'w_branch': out['w_branch'], 'b_gate': out['b_gate'], 'w_out': out['w_out'], 'norm_ffn': out['norm_ffn'], 'ffn_w_gate': out['ffn_w_gate'], 'ffn_w_up': out['ffn_w_up'], 'ffn_w_down': out['ffn_w_down'], 'norm_final': out['norm_final'], 'loss_target': out['loss_target'], 'm_norm_mix': out['m_norm_mix'], 'm_w_in': out['m_w_in'], 'm_mla_q_norm': out['m_mla_q_norm'], 'm_mla_w_uq': out['m_mla_w_uq'], 'm_mla_kv_norm': out['m_mla_kv_norm'], 'm_mla_w_ukv': out['m_mla_w_ukv'], 'm_lru_conv_w': out['m_lru_conv_w'], 'm_lru_conv_b': out['m_lru_conv_b'], 'm_lru_w_a': out['m_lru_w_a'], 'm_lru_b_a': out['m_lru_b_a'], 'm_lru_w_x': out['m_lru_w_x'], 'm_lru_b_x': out['m_lru_b_x'], 'm_lru_lambda': out['m_lru_lambda'], 'm_gdn_conv_w': out['m_gdn_conv_w'], 'm_gdn_a_log': out['m_gdn_a_log'], 'm_gdn_dt_bias': out['m_gdn_dt_bias'], 'm_gdn_norm': out['m_gdn_norm'], 'm_w_branch': out['m_w_branch'], 'm_b_gate': out['m_b_gate'], 'm_w_out': out['m_w_out'], 'm_norm_ffn': out['m_norm_ffn'], 'm_ffn_w_gate': out['m_ffn_w_gate'], 'm_ffn_w_up': out['m_ffn_w_up'], 'm_ffn_w_down': out['m_ffn_w_down'], 'm_norm_final': out['m_norm_final'], 'v_norm_mix': out['v_norm_mix'], 'v_w_in': out['v_w_in'], 'v_mla_q_norm': out['v_mla_q_norm'], 'v_mla_w_uq': out['v_mla_w_uq'], 'v_mla_kv_norm': out['v_mla_kv_norm'], 'v_mla_w_ukv': out['v_mla_w_ukv'], 'v_lru_conv_w': out['v_lru_conv_w'], 'v_lru_conv_b': out['v_lru_conv_b'], 'v_lru_w_a': out['v_lru_w_a'], 'v_lru_b_a': out['v_lru_b_a'], 'v_lru_w_x': out['v_lru_w_x'], 'v_lru_b_x': out['v_lru_b_x'], 'v_lru_lambda': out['v_lru_lambda'], 'v_gdn_conv_w': out['v_gdn_conv_w'], 'v_gdn_a_log': out['v_gdn_a_log'], 'v_gdn_dt_bias': out['v_gdn_dt_bias'], 'v_gdn_norm': out['v_gdn_norm'], 'v_w_branch': out['v_w_branch'], 'v_b_gate': out['v_b_gate'], 'v_w_out': out['v_w_out'], 'v_norm_ffn': out['v_norm_ffn'], 'v_ffn_w_gate': out['v_ffn_w_gate'], 'v_ffn_w_up': out['v_ffn_w_up'], 'v_ffn_w_down': out['v_ffn_w_down'], 'v_norm_final': out['v_norm_final']}


def _loss(weights, diff, rest, loss_target):
    with _jax.named_scope("forward"):
        args = {**rest, TWIN_DIFF_INPUT: diff, **{k: w.astype(_WEIGHT_DTYPES[k]) for k, w in weights.items()}}
        y = _forward(args)
    with _jax.named_scope("loss_head"):
        err = _jnp.square(y.astype(_jnp.float32) - loss_target)
        return 0.5 * _jnp.sum(_jnp.mean(err, axis=-1)) if err.ndim else 0.5 * err


def _adamw(w, g, m, v):
    m = ADAM_B1 * m + (1.0 - ADAM_B1) * g
    v = ADAM_B2 * v + (1.0 - ADAM_B2) * _jnp.square(g)
    m_hat = m / (1.0 - ADAM_B1 ** ADAM_STEP)
    v_hat = v / (1.0 - ADAM_B2 ** ADAM_STEP)
    delta = -ADAM_LR * (m_hat / (_jnp.sqrt(v_hat) + ADAM_EPS) + ADAM_WD * w)
    return delta, m, v


def reference(x, positions, norm_mix, w_in, mla_q_norm, mla_w_uq, mla_kv_norm, mla_w_ukv, lru_conv_w, lru_conv_b, lru_w_a, lru_b_a, lru_w_x, lru_b_x, lru_lambda, gdn_conv_w, gdn_a_log, gdn_dt_bias, gdn_norm, w_branch, b_gate, w_out, norm_ffn, ffn_w_gate, ffn_w_up, ffn_w_down, norm_final, loss_target, m_norm_mix, m_w_in, m_mla_q_norm, m_mla_w_uq, m_mla_kv_norm, m_mla_w_ukv, m_lru_conv_w, m_lru_conv_b, m_lru_w_a, m_lru_b_a, m_lru_w_x, m_lru_b_x, m_lru_lambda, m_gdn_conv_w, m_gdn_a_log, m_gdn_dt_bias, m_gdn_norm, m_w_branch, m_b_gate, m_w_out, m_norm_ffn, m_ffn_w_gate, m_ffn_w_up, m_ffn_w_down, m_norm_final, v_norm_mix, v_w_in, v_mla_q_norm, v_mla_w_uq, v_mla_kv_norm, v_mla_w_ukv, v_lru_conv_w, v_lru_conv_b, v_lru_w_a, v_lru_b_a, v_lru_w_x, v_lru_b_x, v_lru_lambda, v_gdn_conv_w, v_gdn_a_log, v_gdn_dt_bias, v_gdn_norm, v_w_branch, v_b_gate, v_w_out, v_norm_ffn, v_ffn_w_gate, v_ffn_w_up, v_ffn_w_down, v_norm_final):
    given = dict(x=x, positions=positions, norm_mix=norm_mix, w_in=w_in, mla_q_norm=mla_q_norm, mla_w_uq=mla_w_uq, mla_kv_norm=mla_kv_norm, mla_w_ukv=mla_w_ukv, lru_conv_w=lru_conv_w, lru_conv_b=lru_conv_b, lru_w_a=lru_w_a, lru_b_a=lru_b_a, lru_w_x=lru_w_x, lru_b_x=lru_b_x, lru_lambda=lru_lambda, gdn_conv_w=gdn_conv_w, gdn_a_log=gdn_a_log, gdn_dt_bias=gdn_dt_bias, gdn_norm=gdn_norm, w_branch=w_branch, b_gate=b_gate, w_out=w_out, norm_ffn=norm_ffn, ffn_w_gate=ffn_w_gate, ffn_w_up=ffn_w_up, ffn_w_down=ffn_w_down, norm_final=norm_final, loss_target=loss_target, m_norm_mix=m_norm_mix, m_w_in=m_w_in, m_mla_q_norm=m_mla_q_norm, m_mla_w_uq=m_mla_w_uq, m_mla_kv_norm=m_mla_kv_norm, m_mla_w_ukv=m_mla_w_ukv, m_lru_conv_w=m_lru_conv_w, m_lru_conv_b=m_lru_conv_b, m_lru_w_a=m_lru_w_a, m_lru_b_a=m_lru_b_a, m_lru_w_x=m_lru_w_x, m_lru_b_x=m_lru_b_x, m_lru_lambda=m_lru_lambda, m_gdn_conv_w=m_gdn_conv_w, m_gdn_a_log=m_gdn_a_log, m_gdn_dt_bias=m_gdn_dt_bias, m_gdn_norm=m_gdn_norm, m_w_branch=m_w_branch, m_b_gate=m_b_gate, m_w_out=m_w_out, m_norm_ffn=m_norm_ffn, m_ffn_w_gate=m_ffn_w_gate, m_ffn_w_up=m_ffn_w_up, m_ffn_w_down=m_ffn_w_down, m_norm_final=m_norm_final, v_norm_mix=v_norm_mix, v_w_in=v_w_in, v_mla_q_norm=v_mla_q_norm, v_mla_w_uq=v_mla_w_uq, v_mla_kv_norm=v_mla_kv_norm, v_mla_w_ukv=v_mla_w_ukv, v_lru_conv_w=v_lru_conv_w, v_lru_conv_b=v_lru_conv_b, v_lru_w_a=v_lru_w_a, v_lru_b_a=v_lru_b_a, v_lru_w_x=v_lru_w_x, v_lru_b_x=v_lru_b_x, v_lru_lambda=v_lru_lambda, v_gdn_conv_w=v_gdn_conv_w, v_gdn_a_log=v_gdn_a_log, v_gdn_dt_bias=v_gdn_dt_bias, v_gdn_norm=v_gdn_norm, v_w_branch=v_w_branch, v_b_gate=v_b_gate, v_w_out=v_w_out, v_norm_ffn=v_norm_ffn, v_ffn_w_gate=v_ffn_w_gate, v_ffn_w_up=v_ffn_w_up, v_ffn_w_down=v_ffn_w_down, v_norm_final=v_norm_final)
    weights = {n: given[n] for n in TWIN_WEIGHTS}
    shared = {n: given[n] for n in SHARED_INPUTS}
    per_example = {n: given[n] for n in ['x', 'positions']}
    grad_fn = _jax.value_and_grad(_loss, argnums=(0, 1))

    def one_microbatch(ex, loss_target):
        ex = dict(ex)
        diff = ex.pop(TWIN_DIFF_INPUT)
        return grad_fn(weights, diff, {**shared, **ex}, loss_target)

    if N_MICROBATCH == 1:
        loss, (grad_w, grad_x) = one_microbatch(per_example, given["loss_target"])
    else:
        def body(carry, xs):
            loss_sum, grad_sum = carry
            l_k, (gw_k, gx_k) = one_microbatch(xs[0], xs[1])
            with _jax.named_scope("update"):
                return (loss_sum + l_k, _jax.tree.map(_jnp.add, grad_sum, gw_k)), gx_k

        init = (_jnp.zeros((), _jnp.float32), _jax.tree.map(_jnp.zeros_like, weights))
        (loss, grad_w), grad_x = _jax.lax.scan(body, init, (per_example, given["loss_target"]))
    with _jax.named_scope("update"):
        delta_w, new_m, new_v = {}, {}, {}
        for n in TWIN_WEIGHTS:
            delta_w[n], new_m[n], new_v[n] = _adamw(weights[n], grad_w[n], given["m_" + n], given["v_" + n])
    return (loss, grad_x, *[grad_w[n] for n in TWIN_WEIGHTS], *[delta_w[n] for n in TWIN_WEIGHTS],
            *[new_m[n] for n in TWIN_WEIGHTS], *[new_v[n] for n in TWIN_WEIGHTS])
```

```python
import functools
import math

import jax
import jax.numpy as jnp
from jax import lax
from jax.experimental import pallas as pl
from jax.experimental.pallas import tpu as pltpu

f32 = jnp.float32
bf16 = jnp.bfloat16
MESH = pl.DeviceIdType.MESH

D_MODEL = 2048
DEPTH = 4
EPS = 1e-6
N_BRANCH = 3
BRANCH_WIDTH = 1024
MLA_HEADS = 8
QK_NOPE = 128
QK_ROPE = 64
V_HEAD = 128
Q_LORA = 512
KV_LORA = 512
ROPE_THETA = 10000.0
LRU_WIDTH = 1024
LRU_BLOCKS = 8
LRU_BLOCK_W = 128
LRU_C = 8.0
CONV_W = 4
GDN_HEADS = 8
GDN_DK = 128
GDN_DV = 128
GDN_CHUNK = 64
FFN_HIDDEN = 5632
QK_PAD = 256
ATT_SCALE = (QK_NOPE + QK_ROPE) ** -0.5

ADAM_LR = 0.001
ADAM_B1 = 0.9
ADAM_B2 = 0.999
ADAM_EPS = 1e-08
ADAM_WD = 0.01
ADAM_STEP = 10

N_CHIPS = 4
HALF_LAYERS = DEPTH // 2
LANES = 128
SUBLANES = 8
NEG_BIG = -1e30

SHARDED = ["w_in", "mla_w_uq", "mla_w_ukv", "lru_conv_w", "gdn_conv_w", "w_branch", "b_gate", "w_out",
           "ffn_w_gate", "ffn_w_up", "ffn_w_down"]
SHARD_AXIS = {"w_in": 1, "mla_w_uq": 1, "mla_w_ukv": 1, "lru_conv_w": 1, "gdn_conv_w": 1, "w_branch": 2,
              "b_gate": 1, "w_out": 0, "ffn_w_gate": 1, "ffn_w_up": 1, "ffn_w_down": 0}
WIRE_F32 = ("lru_conv_w", "gdn_conv_w", "b_gate")
REPLICATED = ["norm_mix", "mla_q_norm", "mla_kv_norm", "lru_conv_b", "lru_w_a", "lru_b_a", "lru_w_x", "lru_b_x",
              "lru_lambda", "gdn_a_log", "gdn_dt_bias", "gdn_norm", "norm_ffn", "norm_final"]
WEIGHTS = ["norm_mix", "w_in", "mla_q_norm", "mla_w_uq", "mla_kv_norm", "mla_w_ukv", "lru_conv_w", "lru_conv_b",
           "lru_w_a", "lru_b_a", "lru_w_x", "lru_b_x", "lru_lambda", "gdn_conv_w", "gdn_a_log", "gdn_dt_bias",
           "gdn_norm", "w_branch", "b_gate", "w_out", "norm_ffn", "ffn_w_gate", "ffn_w_up", "ffn_w_down",
           "norm_final"]

IN_PIECES = [("c_q", 512), ("c_kv", 512), ("kra", QK_PAD), ("krb", QK_PAD), ("lru_x", 1024), ("lru_y", 1024),
             ("g_q", 1024), ("g_k", 1024), ("g_v", 1024), ("g_z", 1024), ("g_ba", 128),
             ("gl0", D_MODEL), ("gl1", D_MODEL), ("gl2", D_MODEL), ("pad", 128)]
IN_EXT = sum(w for _, w in IN_PIECES)


def _dot(a, b, dims, precision=None):
    return lax.dot_general(a, b, (dims, ((), ())), precision=precision, preferred_element_type=f32)


NN = ((1,), (0,))
NT = ((1,), (1,))
TN = ((0,), (0,))
HI = lax.Precision.HIGHEST


def _largest_tile(n, cap, quantum):
    if n <= cap:
        return n
    best = None
    for t in range(quantum, cap + 1, quantum):
        if n % t == 0:
            best = t
    assert best is not None, (n, cap, quantum)
    return best


def _params(sem):
    return pltpu.CompilerParams(dimension_semantics=sem)


def _matmul(a, b, *, dims, out_dtype, name):
    if dims == TN:
        kdim, m = a.shape
        kdim2, n = b.shape
    elif dims == NT:
        m, kdim = a.shape
        n, kdim2 = b.shape
    else:
        m, kdim = a.shape
        kdim2, n = b.shape
    assert kdim == kdim2, (a.shape, b.shape, dims)
    tm = _largest_tile(m, 1024, LANES)
    tn = _largest_tile(n, 1536, LANES)
    tk = _largest_tile(kdim, 512, LANES)
    nk = kdim // tk

    def body(a_ref, b_ref, o_ref, acc_ref):
        k = pl.program_id(2)

        @pl.when(k == 0)
        def _():
            acc_ref[...] = jnp.zeros_like(acc_ref)

        acc_ref[...] += _dot(a_ref[...].astype(bf16), b_ref[...].astype(bf16), dims)

        @pl.when(k == nk - 1)
        def _():
            o_ref[...] = acc_ref[...].astype(o_ref.dtype)

    if dims == TN:
        a_spec = pl.BlockSpec((tk, tm), lambda i, j, k: (k, i))
        b_spec = pl.BlockSpec((tk, tn), lambda i, j, k: (k, j))
    elif dims == NT:
        a_spec = pl.BlockSpec((tm, tk), lambda i, j, k: (i, k))
        b_spec = pl.BlockSpec((tn, tk), lambda i, j, k: (j, k))
    else:
        a_spec = pl.BlockSpec((tm, tk), lambda i, j, k: (i, k))
        b_spec = pl.BlockSpec((tk, tn), lambda i, j, k: (k, j))
    return pl.pallas_call(
        body, name=name, grid=(m // tm, n // tn, nk), in_specs=[a_spec, b_spec],
        out_specs=pl.BlockSpec((tm, tn), lambda i, j, k: (i, j)),
        out_shape=jax.ShapeDtypeStruct((m, n), out_dtype),
        scratch_shapes=[pltpu.VMEM((tm, tn), f32)],
        compiler_params=_params(("parallel", "parallel", "arbitrary")))(a, b)


def linear(a, w, name):
    @jax.custom_vjp
    def op(a, w):
        return _matmul(a, w, dims=NN, out_dtype=f32, name=name + "_fwd")

    def fwd(a, w):
        return op(a, w), (a, w)

    def bwd(res, dy):
        a, w = res
        da = _matmul(dy, w, dims=NT, out_dtype=f32, name=name + "_da")
        dw = _matmul(a, dy, dims=TN, out_dtype=w.dtype, name=name + "_dw")
        return da, dw

    op.defvjp(fwd, bwd)
    return op(a, w)


def _row_tile(s, widths):
    budget = 12 * 1024 * 1024
    t = 512
    while t > SUBLANES and 2 * 4 * t * sum(widths) > budget:
        t //= 2
    return min(t, s)


def row_op(name, fn, rows, consts, outs, groups=1):
    s = rows[0][0].shape[0]
    r_arrs = [a for a, _ in rows]
    r_kinds = [k for _, k in rows]
    c_arrs = [a for a, _ in consts]
    c_kinds = [k for _, k in consts]
    r_w = [a.shape[1] // groups if k[0] == "g" else a.shape[1] for a, k in rows]
    o_w = [w for w, _ in outs]
    n_r, n_c, n_o = len(rows), len(consts), len(outs)
    diff_r = [i for i, k in enumerate(r_kinds) if not k.endswith("n")]

    def r_spec(i, tile):
        if r_kinds[i][0] == "g":
            return pl.BlockSpec((tile, r_w[i]), lambda r, g: (r, g))
        return pl.BlockSpec((tile, r_w[i]), lambda r, g: (r, 0))

    def c_spec(i):
        nd = c_arrs[i].ndim
        return pl.BlockSpec(c_arrs[i].shape, lambda r, g, nd=nd: (0,) * nd)

    def o_spec(i, tile):
        return pl.BlockSpec((tile, o_w[i]), lambda r, g: (r, g))

    def c_val(ref, kind, g):
        return ref[g] if kind == "p" else ref[...]

    def run_fwd(r_vals, c_vals):
        tile = _row_tile(s, r_w + o_w)

        def body(*refs):
            g = pl.program_id(1)
            rv = [refs[i][...] for i in range(n_r)]
            cv = [c_val(refs[n_r + i], c_kinds[i], g) for i in range(n_c)]
            res = fn(*rv, *cv)
            for i in range(n_o):
                refs[n_r + n_c + i][...] = res[i].astype(outs[i][1])

        res = pl.pallas_call(
            body, name=name + "_fwd", grid=(s // tile, groups),
            in_specs=[r_spec(i, tile) for i in range(n_r)] + [c_spec(i) for i in range(n_c)],
            out_specs=[o_spec(i, tile) for i in range(n_o)],
            out_shape=[jax.ShapeDtypeStruct((s, groups * o_w[i]), outs[i][1]) for i in range(n_o)],
            compiler_params=_params(("parallel", "arbitrary")))(*r_vals, *c_vals)
        return tuple(res)

    def run_bwd(r_vals, c_vals, cts):
        tile = _row_tile(s, r_w + o_w + o_w + [r_w[i] for i in diff_r])

        def body(*refs):
            r = pl.program_id(0)
            g = pl.program_id(1)
            in_refs = refs[:n_r + n_c + n_o]
            dr_refs = refs[n_r + n_c + n_o:n_r + n_c + n_o + len(diff_r)]
            dc_refs = refs[n_r + n_c + n_o + len(diff_r):]
            rv = [in_refs[i][...] for i in range(n_r)]
            cv = [c_val(in_refs[n_r + i], c_kinds[i], g) for i in range(n_c)]
            ct = tuple(in_refs[n_r + n_c + i][...].astype(f32) for i in range(n_o))

            @pl.when((r == 0) & (g == 0))
            def _():
                for d in dc_refs:
                    d[...] = jnp.zeros_like(d)

            def f(*dv):
                full = list(rv)
                for j, i in enumerate(diff_r):
                    full[i] = dv[j]
                return tuple(o.astype(f32) for o in fn(*full, *dv[len(diff_r):]))

            _, vjp = jax.vjp(f, *[rv[i] for i in diff_r], *cv)
            grads = vjp(ct)
            for j, i in enumerate(diff_r):
                if r_kinds[i][0] == "g" or groups == 1:
                    dr_refs[j][...] = grads[j]
                else:
                    @pl.when(g == 0)
                    def _(j=j):
                        dr_refs[j][...] = grads[j]

                    @pl.when(g > 0)
                    def _(j=j):
                        dr_refs[j][...] += grads[j]
            for i in range(n_c):
                gc = grads[len(diff_r) + i]
                if c_kinds[i] == "p":
                    dc_refs[i][g] += gc
                else:
                    dc_refs[i][...] += gc

        res = pl.pallas_call(
            body, name=name + "_bwd", grid=(s // tile, groups),
            in_specs=[r_spec(i, tile) for i in range(n_r)] + [c_spec(i) for i in range(n_c)]
            + [o_spec(i, tile) for i in range(n_o)],
            out_specs=[r_spec(i, tile) for i in diff_r] + [c_spec(i) for i in range(n_c)],
            out_shape=[jax.ShapeDtypeStruct(r_arrs[i].shape, f32) for i in diff_r]
            + [jax.ShapeDtypeStruct(c.shape, f32) for c in c_arrs],
            compiler_params=_params(("arbitrary", "arbitrary")))(*r_vals, *c_vals, *cts)
        return res[:len(diff_r)], res[len(diff_r):]

    @jax.custom_vjp
    def op(r_vals, c_vals):
        return run_fwd(r_vals, c_vals)

    def fwd(r_vals, c_vals):
        return run_fwd(r_vals, c_vals), (r_vals, c_vals)

    def bwd(res, cts):
        r_vals, c_vals = res
        d_r, d_c = run_bwd(r_vals, c_vals, cts)
        full = [jnp.zeros_like(v) for v in r_vals]
        for j, i in enumerate(diff_r):
            full[i] = d_r[j]
        return tuple(full), tuple(d_c)

    op.defvjp(fwd, bwd)
    return op(tuple(r_arrs), tuple(c_arrs))


def _rms(x, w):
    return x * lax.rsqrt(jnp.mean(x * x, axis=-1, keepdims=True) + EPS) * w


def rms_op(x, w, name, groups=1):
    return row_op(name, lambda x, w: (_rms(x, w),), [(x, "g")], [(w.reshape(1, -1), "c")],
                  [(x.shape[1] // groups, f32)], groups)[0]


def split_cols(x, widths):
    offs = [0]
    for w in widths:
        offs.append(offs[-1] + w)

    @jax.custom_vjp
    def op(x):
        return tuple(x[:, offs[i]:offs[i + 1]] for i in range(len(widths)))

    def fwd(x):
        return op(x), None

    def bwd(_, cts):
        return (jnp.concatenate(cts, axis=1),)

    op.defvjp(fwd, bwd)
    return op(x)


def rope_tables(positions):
    s = positions.shape[0]
    tile = min(s, 512)
    half = QK_ROPE // 2

    def body(p_ref, c_ref, s_ref):
        pos = p_ref[...].astype(f32)
        lane = lax.broadcasted_iota(jnp.int32, (1, QK_PAD), 1)
        idx = ((lane - QK_NOPE) % half).astype(f32)
        inv = jnp.exp(idx * (-math.log(ROPE_THETA) / half))
        ang = pos * inv
        rot = (lane >= QK_NOPE) & (lane < QK_NOPE + QK_ROPE)
        c_ref[...] = jnp.where(rot, jnp.cos(ang), jnp.where(lane < QK_NOPE, 1.0, 0.0))
        s_ref[...] = jnp.where(rot, jnp.sin(ang), 0.0)

    return pl.pallas_call(
        body, name="rope_tables", grid=(s // tile,), in_specs=[pl.BlockSpec((tile, 1), lambda i: (i, 0))],
        out_specs=[pl.BlockSpec((tile, QK_PAD), lambda i: (i, 0))] * 2,
        out_shape=[jax.ShapeDtypeStruct((s, QK_PAD), f32)] * 2,
        compiler_params=_params(("parallel",)))(positions)


def _att_tile(s):
    return min(s, 512)


def _att_fwd(q, k, v):
    s = q.shape[0]
    t = _att_tile(s)
    n = s // t

    def body(q_ref, k_ref, v_ref, o_ref, lse_ref, m_sc, l_sc, acc_sc):
        qi = pl.program_id(1)
        kj = pl.program_id(2)

        @pl.when(kj == 0)
        def _():
            m_sc[...] = jnp.full_like(m_sc, NEG_BIG)
            l_sc[...] = jnp.zeros_like(l_sc)
            acc_sc[...] = jnp.zeros_like(acc_sc)

        @pl.when(kj <= qi)
        def _():
            sc = _dot(q_ref[...], k_ref[...], NT) * ATT_SCALE
            rows = qi * t + lax.broadcasted_iota(jnp.int32, (t, t), 0)
            cols = kj * t + lax.broadcasted_iota(jnp.int32, (t, t), 1)
            sc = jnp.where(cols <= rows, sc, NEG_BIG)
            m_old = m_sc[...]
            m_new = jnp.maximum(m_old, jnp.max(sc, axis=1, keepdims=True))
            alpha = jnp.exp(m_old - m_new)
            p = jnp.exp(sc - m_new)
            l_sc[...] = alpha * l_sc[...] + jnp.sum(p, axis=1, keepdims=True)
            acc_sc[...] = alpha * acc_sc[...] + _dot(p.astype(bf16), v_ref[...], NN)
            m_sc[...] = m_new

        @pl.when(kj == qi)
        def _():
            o_ref[...] = acc_sc[...] / l_sc[...]
            lse_ref[...] = jnp.broadcast_to(m_sc[...] + jnp.log(l_sc[...]), (t, LANES))

    return pl.pallas_call(
        body, name="mla_att_fwd", grid=(MLA_HEADS, n, n),
        in_specs=[pl.BlockSpec((t, QK_PAD), lambda h, i, j: (i, h)),
                  pl.BlockSpec((t, QK_PAD), lambda h, i, j: (jnp.minimum(i, j), h)),
                  pl.BlockSpec((t, V_HEAD), lambda h, i, j: (jnp.minimum(i, j), h))],
        out_specs=[pl.BlockSpec((t, V_HEAD), lambda h, i, j: (i, h)),
                   pl.BlockSpec((t, LANES), lambda h, i, j: (i, h))],
        out_shape=[jax.ShapeDtypeStruct((s, MLA_HEADS * V_HEAD), f32),
                   jax.ShapeDtypeStruct((s, MLA_HEADS * LANES), f32)],
        scratch_shapes=[pltpu.VMEM((t, 1), f32), pltpu.VMEM((t, 1), f32), pltpu.VMEM((t, V_HEAD), f32)],
        compiler_params=_params(("parallel", "parallel", "arbitrary")))(q, k, v)


def _att_probs(q, k, lse, do, o, v, qi, kj, t):
    sc = _dot(q, k, NT) * ATT_SCALE
    rows = qi * t + lax.broadcasted_iota(jnp.int32, (t, t), 0)
    cols = kj * t + lax.broadcasted_iota(jnp.int32, (t, t), 1)
    sc = jnp.where(cols <= rows, sc, NEG_BIG)
    p = jnp.exp(sc - lse[:, :1])
    dp = _dot(do, v, NT)
    delta = jnp.sum(do.astype(f32) * o, axis=1, keepdims=True)
    ds = p * (dp - delta) * ATT_SCALE
    return p, ds


def _att_bwd_kv(q, k, v, o, lse, do):
    s = q.shape[0]
    t = _att_tile(s)
    n = s // t

    def body(q_ref, k_ref, v_ref, o_ref, lse_ref, do_ref, dk_ref, dv_ref, dk_sc, dv_sc):
        kj = pl.program_id(1)
        qi = pl.program_id(2)

        @pl.when(qi == 0)
        def _():
            dk_sc[...] = jnp.zeros_like(dk_sc)
            dv_sc[...] = jnp.zeros_like(dv_sc)

        @pl.when(qi >= kj)
        def _():
            p, ds = _att_probs(q_ref[...], k_ref[...], lse_ref[...], do_ref[...], o_ref[...], v_ref[...], qi, kj, t)
            dv_sc[...] += _dot(p.astype(bf16), do_ref[...], TN)
            dk_sc[...] += _dot(ds.astype(bf16), q_ref[...], TN)

        @pl.when(qi == n - 1)
        def _():
            dk_ref[...] = dk_sc[...]
            dv_ref[...] = dv_sc[...]

    qmap = lambda h, j, i: (jnp.maximum(i, j), h)
    kmap = lambda h, j, i: (j, h)
    return pl.pallas_call(
        body, name="mla_att_bwd_kv", grid=(MLA_HEADS, n, n),
        in_specs=[pl.BlockSpec((t, QK_PAD), qmap), pl.BlockSpec((t, QK_PAD), kmap), pl.BlockSpec((t, V_HEAD), kmap),
                  pl.BlockSpec((t, V_HEAD), qmap), pl.BlockSpec((t, LANES), qmap), pl.BlockSpec((t, V_HEAD), qmap)],
        out_specs=[pl.BlockSpec((t, QK_PAD), kmap), pl.BlockSpec((t, V_HEAD), kmap)],
        out_shape=[jax.ShapeDtypeStruct((s, MLA_HEADS * QK_PAD), f32), jax.ShapeDtypeStruct((s, MLA_HEADS * V_HEAD), f32)],
        scratch_shapes=[pltpu.VMEM((t, QK_PAD), f32), pltpu.VMEM((t, V_HEAD), f32)],
        compiler_params=_params(("parallel", "parallel", "arbitrary")))(q, k, v, o, lse, do)


def _att_bwd_q(q, k, v, o, lse, do):
    s = q.shape[0]
    t = _att_tile(s)
    n = s // t

    def body(q_ref, k_ref, v_ref, o_ref, lse_ref, do_ref, dq_ref, dq_sc):
        qi = pl.program_id(1)
        kj = pl.program_id(2)

        @pl.when(kj == 0)
        def _():
            dq_sc[...] = jnp.zeros_like(dq_sc)

        @pl.when(kj <= qi)
        def _():
            _, ds = _att_probs(q_ref[...], k_ref[...], lse_ref[...], do_ref[...], o_ref[...], v_ref[...], qi, kj, t)
            dq_sc[...] += _dot(ds.astype(bf16), k_ref[...], NN)

        @pl.when(kj == qi)
        def _():
            dq_ref[...] = dq_sc[...]

    qmap = lambda h, i, j: (i, h)
    kmap = lambda h, i, j: (jnp.minimum(i, j), h)
    return pl.pallas_call(
        body, name="mla_att_bwd_q", grid=(MLA_HEADS, n, n),
        in_specs=[pl.BlockSpec((t, QK_PAD), qmap), pl.BlockSpec((t, QK_PAD), kmap), pl.BlockSpec((t, V_HEAD), kmap),
                  pl.BlockSpec((t, V_HEAD), qmap), pl.BlockSpec((t, LANES), qmap), pl.BlockSpec((t, V_HEAD), qmap)],
        out_specs=pl.BlockSpec((t, QK_PAD), qmap),
        out_shape=jax.ShapeDtypeStruct((s, MLA_HEADS * QK_PAD), f32),
        scratch_shapes=[pltpu.VMEM((t, QK_PAD), f32)],
        compiler_params=_params(("parallel", "parallel", "arbitrary")))(q, k, v, o, lse, do)


@jax.custom_vjp
def attention(q, k, v):
    return _att_fwd(q.astype(bf16), k.astype(bf16), v.astype(bf16))[0]


def _attention_fwd(q, k, v):
    qb, kb, vb = q.astype(bf16), k.astype(bf16), v.astype(bf16)
    o, lse = _att_fwd(qb, kb, vb)
    return o, (qb, kb, vb, o, lse)


def _attention_bwd(res, do):
    qb, kb, vb, o, lse = res
    dob = do.astype(bf16)
    dk, dv = _att_bwd_kv(qb, kb, vb, o, lse, dob)
    dq = _att_bwd_q(qb, kb, vb, o, lse, dob)
    return dq, dk, dv


attention.defvjp(_attention_fwd, _attention_bwd)


def _seq_tile(s):
    return min(s, 256)


def _chan_tile(c):
    return _largest_tile(c, 512, LANES)


def _shift_down(ext, sh, t):
    if sh == 0:
        return ext[SUBLANES:]
    return pltpu.roll(ext, sh, axis=0)[SUBLANES:]


def _conv_fwd_call(x, w, b, name):
    s, c = x.shape
    t, cb = _seq_tile(s), _chan_tile(c)
    hb = t // SUBLANES

    def body(x_ref, xp_ref, w_ref, b_ref, y_ref):
        ti = pl.program_id(1)
        prev = jnp.where(ti == 0, 0.0, xp_ref[...])
        ext = jnp.concatenate([prev, x_ref[...]], axis=0)
        acc = jnp.broadcast_to(b_ref[...], (t, cb))
        for kk in range(CONV_W):
            acc = acc + w_ref[kk:kk + 1, :] * _shift_down(ext, CONV_W - 1 - kk, t)
        y_ref[...] = acc

    return pl.pallas_call(
        body, name=name + "_fwd", grid=(c // cb, s // t),
        in_specs=[pl.BlockSpec((t, cb), lambda ci, ti: (ti, ci)),
                  pl.BlockSpec((SUBLANES, cb), lambda ci, ti: (jnp.maximum(ti * hb - 1, 0), ci)),
                  pl.BlockSpec((CONV_W, cb), lambda ci, ti: (0, ci)),
                  pl.BlockSpec((1, cb), lambda ci, ti: (0, ci))],
        out_specs=pl.BlockSpec((t, cb), lambda ci, ti: (ti, ci)),
        out_shape=jax.ShapeDtypeStruct((s, c), f32),
        compiler_params=_params(("parallel", "arbitrary")))(x, x, w, b)


def _conv_bwd_call(x, w, dy, name):
    s, c = x.shape
    t, cb = _seq_tile(s), _chan_tile(c)
    hb = t // SUBLANES
    nt = s // t

    def body(x_ref, xp_ref, w_ref, dy_ref, dyn_ref, dx_ref, dw_ref, db_ref):
        ti = pl.program_id(1)

        @pl.when(ti == 0)
        def _():
            dw_ref[...] = jnp.zeros_like(dw_ref)
            db_ref[...] = jnp.zeros_like(db_ref)

        dy = dy_ref[...]
        nxt = jnp.where(ti == nt - 1, 0.0, dyn_ref[...])
        dext = jnp.concatenate([dy, nxt], axis=0)
        prev = jnp.where(ti == 0, 0.0, xp_ref[...])
        xext = jnp.concatenate([prev, x_ref[...]], axis=0)
        dx = jnp.zeros((t, cb), f32)
        for kk in range(CONV_W):
            sh = CONV_W - 1 - kk
            up = dext[:t] if sh == 0 else pltpu.roll(dext, t + SUBLANES - sh, axis=0)[:t]
            dx = dx + w_ref[kk:kk + 1, :] * up
            dw_ref[kk:kk + 1, :] += jnp.sum(dy * _shift_down(xext, sh, t), axis=0, keepdims=True)
        dx_ref[...] = dx
        db_ref[...] += jnp.sum(dy, axis=0, keepdims=True)

    return pl.pallas_call(
        body, name=name + "_bwd", grid=(c // cb, nt),
        in_specs=[pl.BlockSpec((t, cb), lambda ci, ti: (ti, ci)),
                  pl.BlockSpec((SUBLANES, cb), lambda ci, ti: (jnp.maximum(ti * hb - 1, 0), ci)),
                  pl.BlockSpec((CONV_W, cb), lambda ci, ti: (0, ci)),
                  pl.BlockSpec((t, cb), lambda ci, ti: (ti, ci)),
                  pl.BlockSpec((SUBLANES, cb), lambda ci, ti: (jnp.minimum((ti + 1) * hb, nt * hb - 1), ci))],
        out_specs=[pl.BlockSpec((t, cb), lambda ci, ti: (ti, ci)),
                   pl.BlockSpec((CONV_W, cb), lambda ci, ti: (0, ci)),
                   pl.BlockSpec((1, cb), lambda ci, ti: (0, ci))],
        out_shape=[jax.ShapeDtypeStruct((s, c), f32), jax.ShapeDtypeStruct((CONV_W, c), f32),
                   jax.ShapeDtypeStruct((1, c), f32)],
        compiler_params=_params(("parallel", "arbitrary")))(x, x, w, dy, dy)


def causal_conv(x, w, b, name):
    @jax.custom_vjp
    def op(x, w, b):
        return _conv_fwd_call(x, w, b, name)

    def fwd(x, w, b):
        return op(x, w, b), (x, w)

    def bwd(res, dy):
        x, w = res
        dx, dw, db = _conv_bwd_call(x, w, dy, name)
        return dx, dw, db

    op.defvjp(fwd, bwd)
    return op(x, w, b)


def _scan_fwd_call(a, b):
    s, c = a.shape
    t, cb = _seq_tile(s), _chan_tile(c)

    def body(a_ref, b_ref, h_ref, carry):
        ti = pl.program_id(1)

        @pl.when(ti == 0)
        def _():
            carry[...] = jnp.zeros_like(carry)

        av, bv = a_ref[...], b_ref[...]
        rows = lax.broadcasted_iota(jnp.int32, (t, cb), 0)
        d = 1
        while d < t:
            a_sh = jnp.where(rows >= d, pltpu.roll(av, d, axis=0), 1.0)
            b_sh = jnp.where(rows >= d, pltpu.roll(bv, d, axis=0), 0.0)
            bv = av * b_sh + bv
            av = av * a_sh
            d *= 2
        h = av * carry[0:1, :] + bv
        h_ref[...] = h
        carry[0:1, :] = h[t - 1:t, :]

    return pl.pallas_call(
        body, name="lru_scan_fwd", grid=(c // cb, s // t),
        in_specs=[pl.BlockSpec((t, cb), lambda ci, ti: (ti, ci))] * 2,
        out_specs=pl.BlockSpec((t, cb), lambda ci, ti: (ti, ci)),
        out_shape=jax.ShapeDtypeStruct((s, c), f32),
        scratch_shapes=[pltpu.VMEM((SUBLANES, cb), f32)],
        compiler_params=_params(("parallel", "arbitrary")))(a, b)


def _scan_bwd_call(a, h, dh):
    s, c = a.shape
    t, cb = _seq_tile(s), _chan_tile(c)
    hb = t // SUBLANES
    nt = s // t

    def body(a_ref, an_ref, h_ref, hp_ref, dh_ref, da_ref, db_ref, carry):
        step = pl.program_id(1)
        ti = nt - 1 - step

        @pl.when(step == 0)
        def _():
            carry[...] = jnp.zeros_like(carry)

        rows = lax.broadcasted_iota(jnp.int32, (t, cb), 0)
        av = a_ref[...]
        an = jnp.where(rows == t - 1, an_ref[0:1, :], pltpu.roll(av, t - 1, axis=0))
        gv = dh_ref[...]
        d = 1
        while d < t:
            a_sh = jnp.where(rows < t - d, pltpu.roll(an, t - d, axis=0), 1.0)
            g_sh = jnp.where(rows < t - d, pltpu.roll(gv, t - d, axis=0), 0.0)
            gv = an * g_sh + gv
            an = an * a_sh
            d *= 2
        g = an * carry[0:1, :] + gv
        carry[0:1, :] = g[0:1, :]
        hv = h_ref[...]
        first = jnp.where(ti == 0, 0.0, hp_ref[SUBLANES - 1:SUBLANES, :])
        h_prev = jnp.where(rows == 0, first, pltpu.roll(hv, 1, axis=0))
        da_ref[...] = g * h_prev
        db_ref[...] = g

    cur = lambda ci, st: (nt - 1 - st, ci)
    return pl.pallas_call(
        body, name="lru_scan_bwd", grid=(c // cb, nt),
        in_specs=[pl.BlockSpec((t, cb), cur),
                  pl.BlockSpec((SUBLANES, cb), lambda ci, st: (jnp.minimum((nt - st) * hb, nt * hb - 1), ci)),
                  pl.BlockSpec((t, cb), cur),
                  pl.BlockSpec((SUBLANES, cb), lambda ci, st: (jnp.maximum((nt - 1 - st) * hb - 1, 0), ci)),
                  pl.BlockSpec((t, cb), cur)],
        out_specs=[pl.BlockSpec((t, cb), cur)] * 2,
        out_shape=[jax.ShapeDtypeStruct((s, c), f32)] * 2,
        scratch_shapes=[pltpu.VMEM((SUBLANES, cb), f32)],
        compiler_params=_params(("parallel", "arbitrary")))(a, a, h, h, dh)


@jax.custom_vjp
def lru_scan(a, b):
    return _scan_fwd_call(a, b)


def _lru_scan_fwd(a, b):
    h = _scan_fwd_call(a, b)
    return h, (a, h)


def _lru_scan_bwd(res, dh):
    a, h = res
    da, db = _scan_bwd_call(a, h, dh)
    return da, db


lru_scan.defvjp(_lru_scan_fwd, _lru_scan_bwd)


def _expm1(x):
    small = x * (1.0 + x / 2.0 * (1.0 + x / 3.0 * (1.0 + x / 4.0 * (1.0 + x / 5.0 * (1.0 + x / 6.0 * (1.0 + x / 7.0))))))
    return jnp.where(jnp.abs(x) < 0.25, small, jnp.exp(x) - 1.0)


def _lru_gates(xc, w_a, b_a, w_x, b_x, lam):
    xb = xc.astype(bf16)
    r = jax.nn.sigmoid(_dot(xb, w_a.astype(bf16), NN) + b_a)
    gi = jax.nn.sigmoid(_dot(xb, w_x.astype(bf16), NN) + b_x)
    log_a = -LRU_C * r * jax.nn.softplus(-lam)
    a = jnp.exp(log_a)
    mult = jnp.sqrt(-_expm1(2.0 * log_a))
    return a, mult * (gi * xc)


@jax.custom_vjp
def _inv_unit_lower(l):
    c = l.shape[0]
    eye = (lax.broadcasted_iota(jnp.int32, (c, c), 0) == lax.broadcasted_iota(jnp.int32, (c, c), 1)).astype(f32)
    p = eye - l
    m = l
    span = 1
    while 2 * span < c:
        m = _dot(m, m, NN, HI)
        p = p + _dot(p, m, NN, HI)
        span *= 2
    return p


def _inv_unit_lower_fwd(l):
    t = _inv_unit_lower(l)
    return t, t


def _inv_unit_lower_bwd(t, dt):
    return (-_dot(_dot(t, dt, TN, HI), t, NT, HI),)


_inv_unit_lower.defvjp(_inv_unit_lower_fwd, _inv_unit_lower_bwd)


def _gdn_chunk(state, q, k, v, gfull, bfull, h):
    c = GDN_CHUNK
    lane = lax.broadcasted_iota(jnp.int32, (c, LANES), 1)
    g = jnp.sum(jnp.where(lane == GDN_HEADS + h, gfull, 0.0), axis=1, keepdims=True)
    beta = jnp.sum(jnp.where(lane == h, bfull, 0.0), axis=1, keepdims=True)
    r = lax.broadcasted_iota(jnp.int32, (c, c), 0)
    cc = lax.broadcasted_iota(jnp.int32, (c, c), 1)
    tri = (r >= cc).astype(f32)
    gc = _dot(tri, jnp.broadcast_to(g, (c, LANES)), NN, HI)
    first_lane = (lane == 0).astype(f32)
    gc_cols = _dot(first_lane, gc, NT, HI)
    sel_rows = (lax.broadcasted_iota(jnp.int32, (LANES, c), 0) == 0).astype(f32)
    gc_rows = _dot(gc, sel_rows, NN, HI)
    decay = jnp.exp(jnp.where(r >= cc, gc_rows - gc_cols, NEG_BIG))
    q = q * GDN_DK ** -0.5
    k_beta = k * beta
    v_beta = v * beta
    egc = jnp.exp(gc)
    kkt = _dot(k_beta, k, NT, HI) * decay
    t = _inv_unit_lower(jnp.where(r > cc, kkt, 0.0))
    u = _dot(t, v_beta, NN, HI)
    w = _dot(t, k_beta * egc, NN, HI)
    qk = jnp.where(r >= cc, _dot(q, k, NT, HI) * decay, 0.0)
    last = (cc == c - 1).astype(f32)
    gl = _dot(last, gc, NN, HI)
    k_tail = k * jnp.exp(gl - gc)
    g_tail = jnp.exp(gl)
    v_new = u - _dot(w, state, NN, HI)
    o = _dot(q * egc, state, NN, HI) + _dot(qk, v_new, NN, HI)
    new_state = state * jnp.concatenate([g_tail, g_tail], axis=0) + _dot(k_tail, v_new, TN, HI)
    return o, new_state


def _gdn_fwd_call(q, k, v, gfull, bfull):
    s = q.shape[0]
    c = GDN_CHUNK
    n = s // c

    def body(q_ref, k_ref, v_ref, g_ref, b_ref, o_ref, st_ref, state):
        ni = pl.program_id(0)
        h = pl.program_id(1)

        @pl.when(ni == 0)
        def _():
            state[h] = jnp.zeros((GDN_DK, GDN_DV), f32)

        s0 = state[h]
        st_ref[0, 0] = s0
        o, s1 = _gdn_chunk(s0, q_ref[...], k_ref[...], v_ref[...], g_ref[...], b_ref[...], h)
        o_ref[...] = o
        state[h] = s1

    hd = pl.BlockSpec((c, LANES), lambda ni, h: (ni, h))
    sh = pl.BlockSpec((c, LANES), lambda ni, h: (ni, 0))
    return pl.pallas_call(
        body, name="gdn_fwd", grid=(n, GDN_HEADS), in_specs=[hd, hd, hd, sh, sh],
        out_specs=[hd, pl.BlockSpec((1, 1, GDN_DK, GDN_DV), lambda ni, h: (ni, h, 0, 0))],
        out_shape=[jax.ShapeDtypeStruct((s, GDN_HEADS * GDN_DV), f32),
                   jax.ShapeDtypeStruct((n, GDN_HEADS, GDN_DK, GDN_DV), f32)],
        scratch_shapes=[pltpu.VMEM((GDN_HEADS, GDN_DK, GDN_DV), f32)],
        compiler_params=_params(("arbitrary", "arbitrary")))(q, k, v, gfull, bfull)


def _gdn_bwd_call(q, k, v, gfull, bfull, states, do):
    s = q.shape[0]
    c = GDN_CHUNK
    n = s // c

    def body(q_ref, k_ref, v_ref, g_ref, b_ref, st_ref, do_ref, dq_ref, dk_ref, dv_ref, dg_ref, db_ref, dstate):
        step = pl.program_id(0)
        h = pl.program_id(1)

        @pl.when(step == 0)
        def _():
            dstate[h] = jnp.zeros((GDN_DK, GDN_DV), f32)

        _, vjp = jax.vjp(lambda st, q, k, v, g, b: _gdn_chunk(st, q, k, v, g, b, h),
                         st_ref[0, 0], q_ref[...], k_ref[...], v_ref[...], g_ref[...], b_ref[...])
        ds0, dq, dk, dv, dg, db = vjp((do_ref[...], dstate[h]))
        dstate[h] = ds0
        dq_ref[...] = dq
        dk_ref[...] = dk
        dv_ref[...] = dv

        @pl.when(h == 0)
        def _():
            dg_ref[...] = dg
            db_ref[...] = db

        @pl.when(h > 0)
        def _():
            dg_ref[...] += dg
            db_ref[...] += db

    hd = pl.BlockSpec((c, LANES), lambda st, h: (n - 1 - st, h))
    sh = pl.BlockSpec((c, LANES), lambda st, h: (n - 1 - st, 0))
    big = jax.ShapeDtypeStruct((s, GDN_HEADS * GDN_DV), f32)
    small = jax.ShapeDtypeStruct((s, LANES), f32)
    return pl.pallas_call(
        body, name="gdn_bwd", grid=(n, GDN_HEADS),
        in_specs=[hd, hd, hd, sh, sh, pl.BlockSpec((1, 1, GDN_DK, GDN_DV), lambda st, h: (n - 1 - st, h, 0, 0)), hd],
        out_specs=[hd, hd, hd, sh, sh], out_shape=[big, big, big, small, small],
        scratch_shapes=[pltpu.VMEM((GDN_HEADS, GDN_DK, GDN_DV), f32)],
        compiler_params=_params(("arbitrary", "arbitrary")))(q, k, v, gfull, bfull, states, do)


@jax.custom_vjp
def gdn_core(q, k, v, gfull, bfull):
    return _gdn_fwd_call(q, k, v, gfull, bfull)[0]


def _gdn_core_fwd(q, k, v, gfull, bfull):
    o, states = _gdn_fwd_call(q, k, v, gfull, bfull)
    return o, (q, k, v, gfull, bfull, states)


def _gdn_core_bwd(res, do):
    return tuple(_gdn_bwd_call(*res, do))


gdn_core.defvjp(_gdn_core_fwd, _gdn_core_bwd)


def _l2norm(t):
    return t * lax.rsqrt(jnp.sum(t * t, axis=-1, keepdims=True) + EPS)


def _gdn_pre(qc, kc, vc):
    return _l2norm(jax.nn.silu(qc)), _l2norm(jax.nn.silu(kc)), jax.nn.silu(vc)


def _gdn_gates(gba, a_log, dt_bias):
    beta = jax.nn.sigmoid(gba)
    g = -jnp.exp(a_log) * jax.nn.softplus(gba + dt_bias)
    return beta, g


def _gdn_post(o, z, w):
    return (_rms(o, w) * jax.nn.silu(z),)


def _merge(u0, u1, u2, g0, g1, g2, b0, b1, b2):
    return (jax.nn.sigmoid(g0 + b0) * u0 + jax.nn.sigmoid(g1 + b1) * u1 + jax.nn.sigmoid(g2 + b2) * u2,)


def _rope_q(qa, qb, ctab, stab):
    return (qa * ctab + qb * stab,)


def _rope_k(kp, kra, krb, ctab, stab):
    return (kp + kra * ctab + krb * stab,)


def _swap_rot(w):
    half = QK_ROPE // 2
    return jnp.concatenate([-w[..., half:], w[..., :half]], axis=-1)


def _ext_w_in(w):
    d = w.shape[0]
    z = lambda n: jnp.zeros((d, n), w.dtype)
    kr = w[:, 1024:1088]
    return jnp.concatenate([
        w[:, :1024], z(QK_NOPE), kr, z(64), z(QK_NOPE), _swap_rot(kr), z(64),
        w[:, 1088:7232], w[:, 7232:7248], z(112), w[:, 7248:], z(128)], axis=1)


def _ext_w_q(w):
    d = w.shape[0]
    w3 = w.reshape(d, MLA_HEADS, QK_NOPE + QK_ROPE)
    nope, pe = w3[..., :QK_NOPE], w3[..., QK_NOPE:]
    z64 = jnp.zeros((d, MLA_HEADS, 64), w.dtype)
    z128 = jnp.zeros((d, MLA_HEADS, QK_NOPE), w.dtype)
    a = jnp.concatenate([nope, pe, z64], axis=-1).reshape(d, MLA_HEADS * QK_PAD)
    b = jnp.concatenate([z128, _swap_rot(pe), z64], axis=-1).reshape(d, MLA_HEADS * QK_PAD)
    return jnp.concatenate([a, b], axis=1)


def _ext_w_kv(w):
    d = w.shape[0]
    w3 = w.reshape(d, MLA_HEADS, QK_NOPE + V_HEAD)
    kn, v = w3[..., :QK_NOPE], w3[..., QK_NOPE:]
    kp = jnp.concatenate([kn, jnp.zeros_like(kn)], axis=-1).reshape(d, MLA_HEADS * QK_PAD)
    return jnp.concatenate([kp, v.reshape(d, MLA_HEADS * V_HEAD)], axis=1)


def _lane_pad(vec8):
    return jnp.concatenate([jnp.zeros((8,), f32), vec8, jnp.zeros((LANES - 16,), f32)]).reshape(1, LANES)


def layer(x, p, ctab, stab, li):
    tag = f"l{li}_"
    h = rms_op(x, p["norm_mix"], tag + "norm_mix")
    proj = linear(h, _ext_w_in(p["w_in"]), tag + "w_in")
    (c_q, c_kv, kra, krb, lru_x, lru_y, g_q, g_k, g_v, g_z, g_ba, gl0, gl1, gl2, _) = split_cols(
        proj, [w for _, w in IN_PIECES])

    cqn = rms_op(c_q, p["mla_q_norm"], tag + "q_norm")
    ckvn = rms_op(c_kv, p["mla_kv_norm"], tag + "kv_norm")
    qall = linear(cqn, _ext_w_q(p["mla_w_uq"]), tag + "w_uq")
    kvall = linear(ckvn, _ext_w_kv(p["mla_w_ukv"]), tag + "w_ukv")
    qa, qb = split_cols(qall, [MLA_HEADS * QK_PAD] * 2)
    kp, v = split_cols(kvall, [MLA_HEADS * QK_PAD, MLA_HEADS * V_HEAD])
    q = row_op(tag + "rope_q", _rope_q, [(qa, "g"), (qb, "g"), (ctab, "sn"), (stab, "sn")], [], [(QK_PAD, f32)],
               MLA_HEADS)[0]
    k = row_op(tag + "rope_k", _rope_k, [(kp, "g"), (kra, "s"), (krb, "s"), (ctab, "sn"), (stab, "sn")], [],
               [(QK_PAD, f32)], MLA_HEADS)[0]
    y_mla = attention(q, k, v)

    xc = causal_conv(lru_x, p["lru_conv_w"], p["lru_conv_b"].reshape(1, -1), tag + "lru_conv")
    gshape = (LRU_BLOCKS, 1, LRU_BLOCK_W)
    a, bx = row_op(tag + "lru_gates", _lru_gates, [(xc, "g")],
                   [(p["lru_w_a"], "p"), (p["lru_b_a"].reshape(gshape), "p"), (p["lru_w_x"], "p"),
                    (p["lru_b_x"].reshape(gshape), "p"), (p["lru_lambda"].reshape(gshape), "p")],
                   [(LRU_BLOCK_W, f32)] * 2, LRU_BLOCKS)
    hs = lru_scan(a, bx)
    y_lru = row_op(tag + "lru_out", lambda hh, yy: (hh * jax.nn.gelu(yy),), [(hs, "g"), (lru_y, "g")], [],
                   [(LRU_WIDTH, f32)])[0]

    cw = p["gdn_conv_w"]
    nob = jnp.zeros((1, GDN_HEADS * GDN_DK), f32)
    qc = causal_conv(g_q, cw[:, :1024], nob, tag + "gdn_conv_q")
    kc = causal_conv(g_k, cw[:, 1024:2048], nob, tag + "gdn_conv_k")
    vc = causal_conv(g_v, cw[:, 2048:], nob, tag + "gdn_conv_v")
    qn, kn, vs = row_op(tag + "gdn_pre", _gdn_pre, [(qc, "g"), (kc, "g"), (vc, "g")], [], [(GDN_DK, f32)] * 3,
                        GDN_HEADS)
    bfull, gfull = row_op(tag + "gdn_gates", _gdn_gates, [(g_ba, "g")],
                          [(_lane_pad(p["gdn_a_log"]), "c"), (_lane_pad(p["gdn_dt_bias"]), "c")], [(LANES, f32)] * 2)
    o = gdn_core(qn, kn, vs, gfull, bfull)
    y_gdn = row_op(tag + "gdn_post", _gdn_post, [(o, "g"), (g_z, "g")], [(p["gdn_norm"].reshape(1, -1), "c")],
                   [(GDN_DV, f32)], GDN_HEADS)[0]

    wb = p["w_branch"]
    u0 = linear(y_mla, wb[0], tag + "w_branch0")
    u1 = linear(y_lru, wb[1], tag + "w_branch1")
    u2 = linear(y_gdn, wb[2], tag + "w_branch2")
    mixed = row_op(tag + "merge", _merge, [(u0, "g"), (u1, "g"), (u2, "g"), (gl0, "g"), (gl1, "g"), (gl2, "g")],
                   [(p["b_gate"][nb:nb + 1], "c") for nb in range(N_BRANCH)], [(D_MODEL, f32)])[0]
    x = x + linear(mixed, p["w_out"], tag + "w_out")

    h2 = rms_op(x, p["norm_ffn"], tag + "norm_ffn")
    gt = linear(h2, p["ffn_w_gate"], tag + "ffn_gate")
    up = linear(h2, p["ffn_w_up"], tag + "ffn_up")
    act = row_op(tag + "swiglu", lambda a_, b_: (jax.nn.silu(a_) * b_,), [(gt, "g"), (up, "g")], [],
                 [(FFN_HIDDEN, f32)])[0]
    return x + linear(act, p["ffn_w_down"], tag + "ffn_down")


def trunk(layers, x, ctab, stab):
    for li, p in enumerate(layers):
        x = layer(x, p, ctab, stab, li)
    return x


def loss_head(x, target, w):
    s, d = x.shape
    tile = _row_tile(s, [d] * 3)

    def fn(xv, wv, tv):
        err = jnp.square(_rms(xv, wv) - tv)
        return 0.5 * jnp.sum(jnp.mean(err, axis=-1, keepdims=True), axis=0, keepdims=True)

    def body(x_ref, t_ref, w_ref, loss_ref, dx_ref, dw_ref):
        @pl.when(pl.program_id(0) == 0)
        def _():
            loss_ref[...] = jnp.zeros_like(loss_ref)
            dw_ref[...] = jnp.zeros_like(dw_ref)

        tv = t_ref[...]
        val, vjp = jax.vjp(lambda xv, wv: fn(xv, wv, tv), x_ref[...], w_ref[...])
        dx, dw = vjp(jnp.ones((1, 1), f32))
        loss_ref[...] += jnp.broadcast_to(val, loss_ref.shape)
        dx_ref[...] = dx
        dw_ref[...] += dw

    loss, dx, dw = pl.pallas_call(
        body, name="loss_head", grid=(s // tile,),
        in_specs=[pl.BlockSpec((tile, d), lambda i: (i, 0)), pl.BlockSpec((tile, d), lambda i: (i, 0)),
                  pl.BlockSpec((1, d), lambda i: (0, 0))],
        out_specs=[pl.BlockSpec((SUBLANES, LANES), lambda i: (0, 0)), pl.BlockSpec((tile, d), lambda i: (i, 0)),
                   pl.BlockSpec((1, d), lambda i: (0, 0))],
        out_shape=[jax.ShapeDtypeStruct((SUBLANES, LANES), f32), jax.ShapeDtypeStruct((s, d), f32),
                   jax.ShapeDtypeStruct((1, d), f32)],
        compiler_params=_params(("arbitrary",)))(x, target, w.reshape(1, d))
    return loss[0, 0], dx, dw.reshape(d)


def local_step(layers, norm_final, x, positions, target):
    ctab, stab = rope_tables(positions.reshape(-1, 1))
    y, pull = jax.vjp(lambda ls, xx: trunk(ls, xx, ctab, stab), layers, x)
    loss, dy, d_final = loss_head(y, target, norm_final)
    d_layers, dx = pull(dy)
    return loss, dx, d_layers, d_final


def _flat2d(a):
    return a.reshape(-1, a.shape[-1])


def _ew_tile(rows, cols, n_arrays):
    budget = 16 * 1024 * 1024
    cap = max(SUBLANES, budget // (2 * 4 * cols * n_arrays))
    if rows <= cap:
        return rows
    return _largest_tile(rows, cap, SUBLANES)


def elementwise(name, fn, arrays, out_dtypes):
    shape = arrays[0].shape
    flat = [_flat2d(a) for a in arrays]
    rows, cols = flat[0].shape
    tile = _ew_tile(rows, cols, len(arrays) + len(out_dtypes))
    n_in = len(arrays)

    def body(*refs):
        res = fn(*[r[...] for r in refs[:n_in]])
        for o_ref, o in zip(refs[n_in:], res):
            o_ref[...] = o.astype(o_ref.dtype)

    spec = pl.BlockSpec((tile, cols), lambda i: (i, 0))
    res = pl.pallas_call(
        body, name=name, grid=(rows // tile,), in_specs=[spec] * n_in, out_specs=[spec] * len(out_dtypes),
        out_shape=[jax.ShapeDtypeStruct((rows, cols), dt) for dt in out_dtypes],
        compiler_params=_params(("parallel",)))(*flat)
    return [r.reshape(shape) for r in res]


def _adamw(w, g, m, v):
    m = ADAM_B1 * m + (1.0 - ADAM_B1) * g
    v = ADAM_B2 * v + (1.0 - ADAM_B2) * jnp.square(g)
    m_hat = m / (1.0 - ADAM_B1 ** ADAM_STEP)
    v_hat = v / (1.0 - ADAM_B2 ** ADAM_STEP)
    delta = -ADAM_LR * (m_hat / (jnp.sqrt(v_hat) + ADAM_EPS) + ADAM_WD * w)
    return delta, m, v


def _place():
    x, y, c = lax.axis_index("x"), lax.axis_index("y"), lax.axis_index("c")
    return x, y, c, 2 * x + y


CHIP_FLIPS = ((1, 0), (0, 1), (1, 1))


def _hbm_specs(n):
    return [pl.BlockSpec(memory_space=pl.ANY)] * n


def gather_weights(shards, name):
    n = len(shards)

    def body(*refs):
        src = refs[:n]
        out = refs[n:2 * n]
        send1, recv1, send2, recv2, loc = refs[2 * n:]
        x, y, c, j = _place()
        mine = pl.ds(c * HALF_LAYERS, HALF_LAYERS)
        other = pl.ds((1 - c) * HALF_LAYERS, HALF_LAYERS)
        locals_ = []
        for a in range(n):
            for hh, sl in ((c, mine), (1 - c, other)):
                cp = pltpu.make_async_copy(src[a].at[sl], out[a].at[hh, j], loc.at[2 * a + (0 if sl is mine else 1)])
                cp.start()
                locals_.append(cp)
        firsts = []
        for a in range(n):
            for kk, (fx, fy) in enumerate(CHIP_FLIPS):
                cp = pltpu.make_async_remote_copy(
                    src_ref=src[a].at[mine], dst_ref=out[a].at[c, j], send_sem=send1.at[3 * a + kk],
                    recv_sem=recv1.at[3 * a + kk], device_id=(x ^ fx, y ^ fy, c), device_id_type=MESH)
                cp.start()
                firsts.append(cp)
        passed = []
        for a in range(n):
            for kk, (fx, fy) in enumerate(CHIP_FLIPS):
                jp = j ^ (2 * fx + fy)
                landed = out[a].at[c, jp]
                pltpu.make_async_remote_copy(
                    src_ref=landed, dst_ref=landed, send_sem=send1.at[3 * a + kk], recv_sem=recv1.at[3 * a + kk],
                    device_id=(x ^ fx, y ^ fy, c), device_id_type=MESH).wait_recv()
                cp = pltpu.make_async_remote_copy(
                    src_ref=landed, dst_ref=landed, send_sem=send2.at[3 * a + kk], recv_sem=recv2.at[3 * a + kk],
                    device_id=(x, y, 1 - c), device_id_type=MESH)
                cp.start()
                passed.append(cp)
        for a in range(n):
            for kk, (fx, fy) in enumerate(CHIP_FLIPS):
                jp = j ^ (2 * fx + fy)
                theirs = out[a].at[1 - c, jp]
                pltpu.make_async_remote_copy(
                    src_ref=theirs, dst_ref=theirs, send_sem=send2.at[3 * a + kk], recv_sem=recv2.at[3 * a + kk],
                    device_id=(x, y, 1 - c), device_id_type=MESH).wait_recv()
        for cp in firsts + passed:
            cp.wait_send()
        for cp in locals_:
            cp.wait()

    out_shape = [jax.ShapeDtypeStruct((2, N_CHIPS, HALF_LAYERS) + s.shape[1:], s.dtype) for s in shards]
    return pl.pallas_call(
        body, name=name, in_specs=_hbm_specs(n), out_specs=_hbm_specs(n), out_shape=out_shape,
        scratch_shapes=[pltpu.SemaphoreType.DMA((3 * n,)), pltpu.SemaphoreType.DMA((3 * n,)),
                        pltpu.SemaphoreType.DMA((3 * n,)), pltpu.SemaphoreType.DMA((3 * n,)),
                        pltpu.SemaphoreType.DMA((2 * n,))])(*shards)


def swap_with_sibling(arrays, name, lead_other_half=False):
    n = len(arrays)

    def body(*refs):
        src = refs[:n]
        out = refs[n:2 * n]
        send, recv = refs[2 * n:]
        x, y, c, _ = _place()
        cps = []
        for a in range(n):
            s_ref = src[a].at[:, pl.ds((1 - c) * HALF_LAYERS, HALF_LAYERS)] if lead_other_half else src[a]
            cp = pltpu.make_async_remote_copy(src_ref=s_ref, dst_ref=out[a], send_sem=send.at[a], recv_sem=recv.at[a],
                                              device_id=(x, y, 1 - c), device_id_type=MESH)
            cp.start()
            cps.append(cp)
        for cp in cps:
            cp.wait()

    if lead_other_half:
        out_shape = [jax.ShapeDtypeStruct((N_CHIPS, HALF_LAYERS) + a.shape[2:], a.dtype) for a in arrays]
    else:
        out_shape = [jax.ShapeDtypeStruct(a.shape, a.dtype) for a in arrays]
    return pl.pallas_call(
        body, name=name, in_specs=_hbm_specs(n), out_specs=_hbm_specs(n), out_shape=out_shape,
        scratch_shapes=[pltpu.SemaphoreType.DMA((n,)), pltpu.SemaphoreType.DMA((n,))])(*arrays)


def scatter_to_chips(arrays, name):
    n = len(arrays)

    def body(*refs):
        src = refs[:n]
        out = refs[n:2 * n]
        send, recv = refs[2 * n:]
        x, y, c, j = _place()
        cps = []
        for a in range(n):
            for kk, (fx, fy) in enumerate(CHIP_FLIPS):
                cp = pltpu.make_async_remote_copy(
                    src_ref=src[a].at[j ^ (2 * fx + fy)], dst_ref=out[a].at[kk], send_sem=send.at[3 * a + kk],
                    recv_sem=recv.at[3 * a + kk], device_id=(x ^ fx, y ^ fy, c), device_id_type=MESH)
                cp.start()
                cps.append(cp)
        for cp in cps:
            cp.wait()

    out_shape = [jax.ShapeDtypeStruct((3,) + a.shape[1:], a.dtype) for a in arrays]
    return pl.pallas_call(
        body, name=name, in_specs=_hbm_specs(n), out_specs=_hbm_specs(n), out_shape=out_shape,
        scratch_shapes=[pltpu.SemaphoreType.DMA((3 * n,)), pltpu.SemaphoreType.DMA((3 * n,))])(*arrays)


def share_halves(arrays, name):
    n = len(arrays)

    def body(*refs):
        src = refs[:n]
        out = refs[n:2 * n]
        send, recv, loc = refs[2 * n:]
        x, y, c, _ = _place()
        cps, lcs = [], []
        for a in range(n):
            lc = pltpu.make_async_copy(src[a], out[a].at[c], loc.at[a])
            lc.start()
            lcs.append(lc)
            cp = pltpu.make_async_remote_copy(src_ref=src[a], dst_ref=out[a].at[c], send_sem=send.at[a],
                                              recv_sem=recv.at[a], device_id=(x, y, 1 - c), device_id_type=MESH)
            cp.start()
            cps.append(cp)
        for a in range(n):
            cps[a].wait_send()
            theirs = out[a].at[1 - c]
            pltpu.make_async_remote_copy(src_ref=theirs, dst_ref=theirs, send_sem=send.at[a], recv_sem=recv.at[a],
                                         device_id=(x, y, 1 - c), device_id_type=MESH).wait_recv()
            lcs[a].wait()

    out_shape = [jax.ShapeDtypeStruct((2,) + a.shape, a.dtype) for a in arrays]
    return pl.pallas_call(
        body, name=name, in_specs=_hbm_specs(n), out_specs=_hbm_specs(n), out_shape=out_shape,
        scratch_shapes=[pltpu.SemaphoreType.DMA((n,)), pltpu.SemaphoreType.DMA((n,)),
                        pltpu.SemaphoreType.DMA((n,))])(*arrays)


def gather_all_devices(vec, name):
    def body(src, out, send, recv, loc):
        x, y, c, _ = _place()
        me = 4 * x + 2 * y + c
        lc = pltpu.make_async_copy(src, out.at[me], loc.at[0])
        lc.start()
        cps = []
        for mask in range(1, 8):
            fx, fy, fc = (mask >> 2) & 1, (mask >> 1) & 1, mask & 1
            cp = pltpu.make_async_remote_copy(src_ref=src, dst_ref=out.at[me], send_sem=send.at[mask - 1],
                                              recv_sem=recv.at[mask - 1], device_id=(x ^ fx, y ^ fy, c ^ fc),
                                              device_id_type=MESH)
            cp.start()
            cps.append(cp)
        for mask in range(1, 8):
            cps[mask - 1].wait_send()
            theirs = out.at[me ^ mask]
            pltpu.make_async_remote_copy(src_ref=theirs, dst_ref=theirs, send_sem=send.at[mask - 1],
                                         recv_sem=recv.at[mask - 1], device_id=(x, y, c), device_id_type=MESH).wait_recv()
        lc.wait()

    return pl.pallas_call(
        body, name=name, in_specs=_hbm_specs(1), out_specs=pl.BlockSpec(memory_space=pl.ANY),
        out_shape=jax.ShapeDtypeStruct((8,) + vec.shape, vec.dtype),
        scratch_shapes=[pltpu.SemaphoreType.DMA((7,)), pltpu.SemaphoreType.DMA((7,)), pltpu.SemaphoreType.DMA((1,))])(vec)


def _to_chip_major(g, axis):
    shp = g.shape
    g = g.reshape(shp[:axis] + (N_CHIPS, shp[axis] // N_CHIPS) + shp[axis + 1:])
    return jnp.moveaxis(g, axis, 0)


def _from_chip_major(b, axis):
    b = jnp.moveaxis(b, 0, axis)
    shp = b.shape
    return b.reshape(shp[:axis] + (shp[axis] * shp[axis + 1],) + shp[axis + 2:])


def _pack(arrays):
    flat = jnp.concatenate([a.reshape(-1) for a in arrays])
    pad = (-flat.shape[0]) % (256 * LANES)
    return jnp.concatenate([flat, jnp.zeros((pad,), flat.dtype)]).reshape(-1, LANES)


def _unpack(packed, like):
    flat = packed.reshape(-1)
    out, off = [], 0
    for a in like:
        out.append(flat[off:off + a.size].reshape(a.shape))
        off += a.size
    return out


def kernel(x, positions, norm_mix, w_in, mla_q_norm, mla_w_uq, mla_kv_norm, mla_w_ukv, lru_conv_w, lru_conv_b, lru_w_a, lru_b_a, lru_w_x, lru_b_x, lru_lambda, gdn_conv_w, gdn_a_log, gdn_dt_bias, gdn_norm, w_branch, b_gate, w_out, norm_ffn, ffn_w_gate, ffn_w_up, ffn_w_down, norm_final, loss_target, m_norm_mix, m_w_in, m_mla_q_norm, m_mla_w_uq, m_mla_kv_norm, m_mla_w_ukv, m_lru_conv_w, m_lru_conv_b, m_lru_w_a, m_lru_b_a, m_lru_w_x, m_lru_b_x, m_lru_lambda, m_gdn_conv_w, m_gdn_a_log, m_gdn_dt_bias, m_gdn_norm, m_w_branch, m_b_gate, m_w_out, m_norm_ffn, m_ffn_w_gate, m_ffn_w_up, m_ffn_w_down, m_norm_final, v_norm_mix, v_w_in, v_mla_q_norm, v_mla_w_uq, v_mla_kv_norm, v_mla_w_ukv, v_lru_conv_w, v_lru_conv_b, v_lru_w_a, v_lru_b_a, v_lru_w_x, v_lru_b_x, v_lru_lambda, v_gdn_conv_w, v_gdn_a_log, v_gdn_dt_bias, v_gdn_norm, v_w_branch, v_b_gate, v_w_out, v_norm_ffn, v_ffn_w_gate, v_ffn_w_up, v_ffn_w_down, v_norm_final):
    given = dict(locals())
    w = {n: given[n] for n in WEIGHTS}
    m = {n: given["m_" + n] for n in WEIGHTS}
    v = {n: given["v_" + n] for n in WEIGHTS}

    wire = {n: (w[n] if n in WIRE_F32 else elementwise("cast_" + n, lambda a: (a,), [w[n]], [bf16])[0]) for n in SHARDED}
    big = [n for n in SHARDED if n not in WIRE_F32]
    gathered = dict(zip(big, gather_weights([wire[n] for n in big], "gather_big")))
    gathered.update(zip(WIRE_F32, gather_weights([wire[n] for n in WIRE_F32], "gather_small")))
    layers = []
    for li in range(DEPTH):
        p = {n: _from_chip_major(gathered[n][li // HALF_LAYERS, :, li % HALF_LAYERS], SHARD_AXIS[n]) for n in SHARDED}
        p.update({n: w[n][li] for n in REPLICATED if n != "norm_final"})
        layers.append(p)

    loss, grad_x, d_layers, d_final = local_step(layers, norm_final, x[0], positions[0], loss_target[0])
    loss = lax.psum(loss, ("x", "y", "c"))

    grads = {}
    chip_major = {n: jnp.stack([_to_chip_major(d_layers[li][n], SHARD_AXIS[n]) for li in range(DEPTH)], axis=1)
                  for n in SHARDED}
    c = lax.axis_index("c")
    chip = 2 * lax.axis_index("x") + lax.axis_index("y")
    for group, names in (("big", big), ("small", list(WIRE_F32))):
        theirs = swap_with_sibling([chip_major[n] for n in names], "grad_pair_" + group, lead_other_half=True)
        pair16, pair32 = [], []
        for n, t in zip(names, theirs):
            mine = lax.dynamic_slice_in_dim(chip_major[n], c * HALF_LAYERS, HALF_LAYERS, axis=1)
            s16, s32 = elementwise("pair_sum_" + n, lambda a, b: (a.astype(f32) + b.astype(f32),) * 2, [mine, t],
                                   [mine.dtype, f32])
            pair16.append(s16)
            pair32.append(s32)
        landed = scatter_to_chips(pair16, "grad_chips_" + group)
        halves = []
        for n, s32, l in zip(names, pair32, landed):
            own = lax.dynamic_index_in_dim(s32, chip, axis=0, keepdims=False)
            halves.append(elementwise(
                "chip_sum_" + n, lambda o, a, b, d: (o + a.astype(f32) + b.astype(f32) + d.astype(f32),),
                [own, l[0], l[1], l[2]], [f32])[0])
        for n, both in zip(names, share_halves(halves, "grad_share_" + group)):
            grads[n] = both.reshape(w[n].shape)

    rep_g = [jnp.stack([d_layers[li][n] for li in range(DEPTH)]) for n in REPLICATED if n != "norm_final"] + [d_final]
    everyone = gather_all_devices(_pack(rep_g), "grad_replicated")
    total = elementwise("replicated_sum", lambda *a: (functools.reduce(lambda p, q: p + q, a),),
                        [everyone[d] for d in range(8)], [f32])[0]
    grads.update(zip(REPLICATED, _unpack(total, [w[n] for n in REPLICATED])))

    delta, new_m, new_v = {}, {}, {}
    for n in SHARDED:
        delta[n], new_m[n], new_v[n] = elementwise("adamw_" + n, _adamw, [w[n], grads[n], m[n], v[n]], [f32] * 3)
    rep = [n for n in REPLICATED]
    pd, pm, pv = elementwise("adamw_replicated", _adamw,
                             [_pack([w[n] for n in rep]), total, _pack([m[n] for n in rep]), _pack([v[n] for n in rep])],
                             [f32] * 3)
    for dst, packed in ((delta, pd), (new_m, pm), (new_v, pv)):
        dst.update(zip(rep, _unpack(packed, [w[n] for n in rep])))

    return (loss, grad_x[None], *[grads[n] for n in WEIGHTS], *[delta[n] for n in WEIGHTS],
            *[new_m[n] for n in WEIGHTS], *[new_v[n] for n in WEIGHTS])
```

```python
import functools
import math

import jax
import jax.numpy as jnp
from jax import lax
from jax.experimental import pallas as pl
from jax.experimental.pallas import tpu as pltpu

f32 = jnp.float32
bf16 = jnp.bfloat16
MESH = pl.DeviceIdType.MESH

D_MODEL = 2048
DEPTH = 4
EPS = 1e-6
N_BRANCH = 3
BRANCH_WIDTH = 1024
MLA_HEADS = 8
QK_NOPE = 128
QK_ROPE = 64
V_HEAD = 128
Q_LORA = 512
KV_LORA = 512
ROPE_THETA = 10000.0
LRU_WIDTH = 1024
LRU_BLOCKS = 8
LRU_BLOCK_W = 128
LRU_C = 8.0
CONV_W = 4
GDN_HEADS = 8
GDN_DK = 128
GDN_DV = 128
GDN_CHUNK = 64
FFN_HIDDEN = 5632
QK_PAD = 256
ATT_SCALE = (QK_NOPE + QK_ROPE) ** -0.5

ADAM_LR = 0.001
ADAM_B1 = 0.9
ADAM_B2 = 0.999
ADAM_EPS = 1e-08
ADAM_WD = 0.01
ADAM_STEP = 10

N_CHIPS = 4
HALF_LAYERS = DEPTH // 2
LANES = 128
SUBLANES = 8
NEG_BIG = -1e30

SHARDED = ["w_in", "mla_w_uq", "mla_w_ukv", "lru_conv_w", "gdn_conv_w", "w_branch", "b_gate", "w_out",
           "ffn_w_gate", "ffn_w_up", "ffn_w_down"]
SHARD_AXIS = {"w_in": 1, "mla_w_uq": 1, "mla_w_ukv": 1, "lru_conv_w": 1, "gdn_conv_w": 1, "w_branch": 2,
              "b_gate": 1, "w_out": 0, "ffn_w_gate": 1, "ffn_w_up": 1, "ffn_w_down": 0}
WIRE_F32 = ("lru_conv_w", "gdn_conv_w", "b_gate")
REPLICATED = ["norm_mix", "mla_q_norm", "mla_kv_norm", "lru_conv_b", "lru_w_a", "lru_b_a", "lru_w_x", "lru_b_x",
              "lru_lambda", "gdn_a_log", "gdn_dt_bias", "gdn_norm", "norm_ffn", "norm_final"]
WEIGHTS = ["norm_mix", "w_in", "mla_q_norm", "mla_w_uq", "mla_kv_norm", "mla_w_ukv", "lru_conv_w", "lru_conv_b",
           "lru_w_a", "lru_b_a", "lru_w_x", "lru_b_x", "lru_lambda", "gdn_conv_w", "gdn_a_log", "gdn_dt_bias",
           "gdn_norm", "w_branch", "b_gate", "w_out", "norm_ffn", "ffn_w_gate", "ffn_w_up", "ffn_w_down",
           "norm_final"]

IN_PIECES = [("c_q", 512), ("c_kv", 512), ("kra", QK_PAD), ("krb", QK_PAD), ("lru_x", 1024), ("lru_y", 1024),
             ("g_q", 1024), ("g_k", 1024), ("g_v", 1024), ("g_z", 1024), ("g_ba", 128),
             ("gl0", D_MODEL), ("gl1", D_MODEL), ("gl2", D_MODEL), ("pad", 128)]
IN_EXT = sum(w for _, w in IN_PIECES)


def _dot(a, b, dims, precision=None):
    return lax.dot_general(a, b, (dims, ((), ())), precision=precision, preferred_element_type=f32)


NN = ((1,), (0,))
NT = ((1,), (1,))
TN = ((0,), (0,))
HI = lax.Precision.HIGHEST
MID = lax.Precision.HIGH


def _bdot(a, b, dims):
    return _dot(a.astype(bf16), b.astype(bf16), dims)


def _largest_tile(n, cap, quantum):
    if n <= cap:
        return n
    best = None
    for t in range(quantum, cap + 1, quantum):
        if n % t == 0:
            best = t
    assert best is not None, (n, cap, quantum)
    return best


def _params(sem):
    return pltpu.CompilerParams(dimension_semantics=sem)


def _matmul(a, b, *, dims, out_dtype, name):
    if dims == TN:
        kdim, m = a.shape
        kdim2, n = b.shape
    elif dims == NT:
        m, kdim = a.shape
        n, kdim2 = b.shape
    else:
        m, kdim = a.shape
        kdim2, n = b.shape
    assert kdim == kdim2, (a.shape, b.shape, dims)
    tm = _largest_tile(m, 1024, LANES)
    tn = _largest_tile(n, 1536, LANES)
    tk = _largest_tile(kdim, 1536, LANES)
    nk = kdim // tk

    def body(a_ref, b_ref, o_ref, acc_ref):
        k = pl.program_id(2)

        @pl.when(k == 0)
        def _():
            acc_ref[...] = jnp.zeros_like(acc_ref)

        acc_ref[...] += _dot(a_ref[...].astype(bf16), b_ref[...].astype(bf16), dims)

        @pl.when(k == nk - 1)
        def _():
            o_ref[...] = acc_ref[...].astype(o_ref.dtype)

    if dims == TN:
        a_spec = pl.BlockSpec((tk, tm), lambda i, j, k: (k, i))
        b_spec = pl.BlockSpec((tk, tn), lambda i, j, k: (k, j))
    elif dims == NT:
        a_spec = pl.BlockSpec((tm, tk), lambda i, j, k: (i, k))
        b_spec = pl.BlockSpec((tn, tk), lambda i, j, k: (j, k))
    else:
        a_spec = pl.BlockSpec((tm, tk), lambda i, j, k: (i, k))
        b_spec = pl.BlockSpec((tk, tn), lambda i, j, k: (k, j))
    return pl.pallas_call(
        body, name=name, grid=(m // tm, n // tn, nk), in_specs=[a_spec, b_spec],
        out_specs=pl.BlockSpec((tm, tn), lambda i, j, k: (i, j)),
        out_shape=jax.ShapeDtypeStruct((m, n), out_dtype),
        scratch_shapes=[pltpu.VMEM((tm, tn), f32)],
        compiler_params=_params(("parallel", "parallel", "arbitrary")))(a, b)


def linear(a, w, name):
    @jax.custom_vjp
    def op(a, w):
        return _matmul(a, w, dims=NN, out_dtype=f32, name=name + "_fwd")

    def fwd(a, w):
        return op(a, w), (a, w)

    def bwd(res, dy):
        a, w = res
        da = _matmul(dy, w, dims=NT, out_dtype=f32, name=name + "_da")
        dw = _matmul(a, dy, dims=TN, out_dtype=w.dtype, name=name + "_dw")
        return da, dw

    op.defvjp(fwd, bwd)
    return op(a, w)


def _row_tile(s, widths):
    budget = 12 * 1024 * 1024
    t = 512
    while t > SUBLANES and 2 * 4 * t * sum(widths) > budget:
        t //= 2
    return min(t, s)


def row_op(name, fn, rows, consts, outs, groups=1):
    s = rows[0][0].shape[0]
    r_arrs = [a for a, _ in rows]
    r_kinds = [k for _, k in rows]
    c_arrs = [a for a, _ in consts]
    c_kinds = [k for _, k in consts]
    r_w = [a.shape[1] // groups if k[0] == "g" else a.shape[1] for a, k in rows]
    o_w = [w for w, _ in outs]
    n_r, n_c, n_o = len(rows), len(consts), len(outs)
    diff_r = [i for i, k in enumerate(r_kinds) if not k.endswith("n")]

    def r_spec(i, tile):
        if r_kinds[i][0] == "g":
            return pl.BlockSpec((tile, r_w[i]), lambda r, g: (r, g))
        return pl.BlockSpec((tile, r_w[i]), lambda r, g: (r, 0))

    def c_spec(i):
        nd = c_arrs[i].ndim
        return pl.BlockSpec(c_arrs[i].shape, lambda r, g, nd=nd: (0,) * nd)

    def o_spec(i, tile):
        return pl.BlockSpec((tile, o_w[i]), lambda r, g: (r, g))

    def c_val(ref, kind, g):
        return ref[g] if kind == "p" else ref[...]

    def run_fwd(r_vals, c_vals):
        tile = _row_tile(s, r_w + o_w)

        def body(*refs):
            g = pl.program_id(1)
            rv = [refs[i][...] for i in range(n_r)]
            cv = [c_val(refs[n_r + i], c_kinds[i], g) for i in range(n_c)]
            res = fn(*rv, *cv)
            for i in range(n_o):
                refs[n_r + n_c + i][...] = res[i].astype(outs[i][1])

        res = pl.pallas_call(
            body, name=name + "_fwd", grid=(s // tile, groups),
            in_specs=[r_spec(i, tile) for i in range(n_r)] + [c_spec(i) for i in range(n_c)],
            out_specs=[o_spec(i, tile) for i in range(n_o)],
            out_shape=[jax.ShapeDtypeStruct((s, groups * o_w[i]), outs[i][1]) for i in range(n_o)],
            compiler_params=_params(("parallel", "arbitrary")))(*r_vals, *c_vals)
        return tuple(res)

    def run_bwd(r_vals, c_vals, cts):
        tile = _row_tile(s, r_w + o_w + o_w + [r_w[i] for i in diff_r])

        def body(*refs):
            r = pl.program_id(0)
            g = pl.program_id(1)
            in_refs = refs[:n_r + n_c + n_o]
            dr_refs = refs[n_r + n_c + n_o:n_r + n_c + n_o + len(diff_r)]
            dc_refs = refs[n_r + n_c + n_o + len(diff_r):]
            rv = [in_refs[i][...] for i in range(n_r)]
            cv = [c_val(in_refs[n_r + i], c_kinds[i], g) for i in range(n_c)]
            ct = tuple(in_refs[n_r + n_c + i][...].astype(f32) for i in range(n_o))

            @pl.when((r == 0) & (g == 0))
            def _():
                for d in dc_refs:
                    d[...] = jnp.zeros_like(d)

            def f(*dv):
                full = list(rv)
                for j, i in enumerate(diff_r):
                    full[i] = dv[j]
                return tuple(o.astype(f32) for o in fn(*full, *dv[len(diff_r):]))

            _, vjp = jax.vjp(f, *[rv[i] for i in diff_r], *cv)
            grads = vjp(ct)
            for j, i in enumerate(diff_r):
                if r_kinds[i][0] == "g" or groups == 1:
                    dr_refs[j][...] = grads[j]
                else:
                    @pl.when(g == 0)
                    def _(j=j):
                        dr_refs[j][...] = grads[j]

                    @pl.when(g > 0)
                    def _(j=j):
                        dr_refs[j][...] += grads[j]
            for i in range(n_c):
                gc = grads[len(diff_r) + i]
                if c_kinds[i] == "p":
                    dc_refs[i][g] += gc
                else:
                    dc_refs[i][...] += gc

        res = pl.pallas_call(
            body, name=name + "_bwd", grid=(s // tile, groups),
            in_specs=[r_spec(i, tile) for i in range(n_r)] + [c_spec(i) for i in range(n_c)]
            + [o_spec(i, tile) for i in range(n_o)],
            out_specs=[r_spec(i, tile) for i in diff_r] + [c_spec(i) for i in range(n_c)],
            out_shape=[jax.ShapeDtypeStruct(r_arrs[i].shape, f32) for i in diff_r]
            + [jax.ShapeDtypeStruct(c.shape, f32) for c in c_arrs],
            compiler_params=_params(("arbitrary", "arbitrary")))(*r_vals, *c_vals, *cts)
        return res[:len(diff_r)], res[len(diff_r):]

    @jax.custom_vjp
    def op(r_vals, c_vals):
        return run_fwd(r_vals, c_vals)

    def fwd(r_vals, c_vals):
        return run_fwd(r_vals, c_vals), (r_vals, c_vals)

    def bwd(res, cts):
        r_vals, c_vals = res
        d_r, d_c = run_bwd(r_vals, c_vals, cts)
        full = [jnp.zeros_like(v) for v in r_vals]
        for j, i in enumerate(diff_r):
            full[i] = d_r[j]
        return tuple(full), tuple(d_c)

    op.defvjp(fwd, bwd)
    return op(tuple(r_arrs), tuple(c_arrs))


def _rms(x, w):
    return x * lax.rsqrt(jnp.mean(x * x, axis=-1, keepdims=True) + EPS) * w


def rms_op(x, w, name, groups=1):
    return row_op(name, lambda x, w: (_rms(x, w),), [(x, "g")], [(w.reshape(1, -1), "c")],
                  [(x.shape[1] // groups, f32)], groups)[0]


def split_cols(x, widths):
    offs = [0]
    for w in widths:
        offs.append(offs[-1] + w)

    @jax.custom_vjp
    def op(x):
        return tuple(x[:, offs[i]:offs[i + 1]] for i in range(len(widths)))

    def fwd(x):
        return op(x), None

    def bwd(_, cts):
        return (jnp.concatenate(cts, axis=1),)

    op.defvjp(fwd, bwd)
    return op(x)


def rope_tables(positions):
    s = positions.shape[0]
    tile = min(s, 512)
    half = QK_ROPE // 2

    def body(p_ref, c_ref, s_ref):
        pos = p_ref[...].astype(f32)
        lane = lax.broadcasted_iota(jnp.int32, (1, QK_PAD), 1)
        idx = ((lane - QK_NOPE) % half).astype(f32)
        inv = jnp.exp(idx * (-math.log(ROPE_THETA) / half))
        ang = pos * inv
        rot = (lane >= QK_NOPE) & (lane < QK_NOPE + QK_ROPE)
        c_ref[...] = jnp.where(rot, jnp.cos(ang), jnp.where(lane < QK_NOPE, 1.0, 0.0))
        s_ref[...] = jnp.where(rot, jnp.sin(ang), 0.0)

    return pl.pallas_call(
        body, name="rope_tables", grid=(s // tile,), in_specs=[pl.BlockSpec((tile, 1), lambda i: (i, 0))],
        out_specs=[pl.BlockSpec((tile, QK_PAD), lambda i: (i, 0))] * 2,
        out_shape=[jax.ShapeDtypeStruct((s, QK_PAD), f32)] * 2,
        compiler_params=_params(("parallel",)))(positions)


ATT_ROW_GROUPS = 2


def _att_tile(s):
    return min(s, 512)


def _att_fwd(q, k, v):
    s = q.shape[0]
    t = _att_tile(s)
    n = s // t

    qi_tab, kj_tab = _block_pairs(n, by_query=True)

    def body(qi_ref, kj_ref, q_ref, k_ref, v_ref, o_ref, lse_ref, m_sc, l_sc, acc_sc):
        pair = pl.program_id(1)
        qi = qi_ref[pair]
        kj = kj_ref[pair]

        @pl.when(kj == 0)
        def _():
            m_sc[...] = jnp.full_like(m_sc, NEG_BIG)
            l_sc[...] = jnp.zeros_like(l_sc)
            acc_sc[...] = jnp.zeros_like(acc_sc)

        def accumulate(diagonal):
            for rows in _row_groups(t):
                sc = _dot(q_ref[rows, :], k_ref[...], NT) * ATT_SCALE
                if diagonal:
                    sc = _mask_diagonal(sc, rows.start)
                m_old = m_sc[rows, :]
                m_new = jnp.maximum(m_old, jnp.max(sc, axis=1, keepdims=True))
                alpha = jnp.exp(m_old - m_new)
                p = jnp.exp(sc - m_new)
                l_sc[rows, :] = alpha * l_sc[rows, :] + jnp.sum(p, axis=1, keepdims=True)
                acc_sc[rows, :] = alpha * acc_sc[rows, :] + _dot(p.astype(bf16), v_ref[...], NN)
                m_sc[rows, :] = m_new

        @pl.when(kj < qi)
        def _():
            accumulate(False)

        @pl.when(kj == qi)
        def _():
            accumulate(True)
            o_ref[...] = acc_sc[...] / l_sc[...]
            lse_ref[...] = jnp.broadcast_to(m_sc[...] + jnp.log(l_sc[...]), (t, LANES))

    qmap = lambda h, p, qi_ref, kj_ref: (qi_ref[p], h)
    kmap = lambda h, p, qi_ref, kj_ref: (kj_ref[p], h)
    return pl.pallas_call(
        body, name="mla_att_fwd",
        grid_spec=pltpu.PrefetchScalarGridSpec(
            num_scalar_prefetch=2, grid=(MLA_HEADS, len(qi_tab)),
            in_specs=[pl.BlockSpec((t, QK_PAD), qmap), pl.BlockSpec((t, QK_PAD), kmap),
                      pl.BlockSpec((t, V_HEAD), kmap)],
            out_specs=[pl.BlockSpec((t, V_HEAD), qmap), pl.BlockSpec((t, LANES), qmap)],
            scratch_shapes=[pltpu.VMEM((t, 1), f32), pltpu.VMEM((t, 1), f32), pltpu.VMEM((t, V_HEAD), f32)]),
        out_shape=[jax.ShapeDtypeStruct((s, MLA_HEADS * V_HEAD), f32),
                   jax.ShapeDtypeStruct((s, MLA_HEADS * LANES), f32)],
        compiler_params=_params(("parallel", "arbitrary")))(qi_tab, kj_tab, q, k, v)


def _block_pairs(n, by_query):
    if by_query:
        pairs = [(i, j) for i in range(n) for j in range(i + 1)]
    else:
        pairs = [(i, j) for j in range(n) for i in range(j, n)]
    return (jnp.asarray([p[0] for p in pairs], jnp.int32), jnp.asarray([p[1] for p in pairs], jnp.int32))


def _mask_diagonal(sc, row0):
    rows = row0 + lax.broadcasted_iota(jnp.int32, sc.shape, 0)
    cols = lax.broadcasted_iota(jnp.int32, sc.shape, 1)
    return jnp.where(cols <= rows, sc, NEG_BIG)


def _row_groups(t):
    return [slice(rs, rs + t // ATT_ROW_GROUPS) for rs in range(0, t, t // ATT_ROW_GROUPS)]


def _att_probs(qs, k, lses, dos, os_, v, groups, diagonal):
    scs = [_dot(q, k, NT) * ATT_SCALE for q in qs]
    if diagonal:
        scs = [_mask_diagonal(sc, rows.start) for sc, rows in zip(scs, groups)]
    ps = _each(lambda sc, lse: jnp.exp(sc - lse[:, :1]), scs, lses)
    dps = [_dot(do, v, NT) for do in dos]
    deltas = _each(lambda do, o: jnp.sum(do.astype(f32) * o, axis=1, keepdims=True), dos, os_)
    dss = _each(lambda p, dp, delta: p * (dp - delta) * ATT_SCALE, ps, dps, deltas)
    return ps, dss


def _att_bwd_kv(q, k, v, o, lse, do):
    s = q.shape[0]
    t = _att_tile(s)
    n = s // t

    qi_tab, kj_tab = _block_pairs(n, by_query=False)

    def body(qi_ref, kj_ref, q_ref, k_ref, v_ref, o_ref, lse_ref, do_ref, dk_ref, dv_ref, dk_sc, dv_sc):
        pair = pl.program_id(1)
        qi = qi_ref[pair]
        kj = kj_ref[pair]

        def accumulate(diagonal):
            dv = dk = None
            for rows in _row_groups(t):
                (p,), (ds,) = _att_probs([q_ref[rows, :]], k_ref[...], [lse_ref[rows, :]], [do_ref[rows, :]],
                                         [o_ref[rows, :]], v_ref[...], [rows], diagonal)
                dv_g = _dot(p.astype(bf16), do_ref[rows, :], TN)
                dk_g = _dot(ds.astype(bf16), q_ref[rows, :], TN)
                dv = dv_g if dv is None else dv + dv_g
                dk = dk_g if dk is None else dk + dk_g
            return dv, dk

        @pl.when(qi == kj)
        def _():
            dv, dk = accumulate(True)
            dv_sc[...] = dv
            dk_sc[...] = dk

        @pl.when(qi > kj)
        def _():
            dv, dk = accumulate(False)
            dv_sc[...] += dv
            dk_sc[...] += dk

        @pl.when(qi == n - 1)
        def _():
            dk_ref[...] = dk_sc[...]
            dv_ref[...] = dv_sc[...]

    qmap = lambda h, p, qi_ref, kj_ref: (qi_ref[p], h)
    kmap = lambda h, p, qi_ref, kj_ref: (kj_ref[p], h)
    return pl.pallas_call(
        body, name="mla_att_bwd_kv",
        grid_spec=pltpu.PrefetchScalarGridSpec(
            num_scalar_prefetch=2, grid=(MLA_HEADS, len(qi_tab)),
            in_specs=[pl.BlockSpec((t, QK_PAD), qmap), pl.BlockSpec((t, QK_PAD), kmap), pl.BlockSpec((t, V_HEAD), kmap),
                      pl.BlockSpec((t, V_HEAD), qmap), pl.BlockSpec((t, LANES), qmap), pl.BlockSpec((t, V_HEAD), qmap)],
            out_specs=[pl.BlockSpec((t, QK_PAD), kmap), pl.BlockSpec((t, V_HEAD), kmap)],
            scratch_shapes=[pltpu.VMEM((t, QK_PAD), f32), pltpu.VMEM((t, V_HEAD), f32)]),
        out_shape=[jax.ShapeDtypeStruct((s, MLA_HEADS * QK_PAD), f32), jax.ShapeDtypeStruct((s, MLA_HEADS * V_HEAD), f32)],
        compiler_params=_params(("parallel", "arbitrary")))(qi_tab, kj_tab, q, k, v, o, lse, do)


def _att_bwd_q(q, k, v, o, lse, do):
    s = q.shape[0]
    t = _att_tile(s)
    n = s // t

    qi_tab, kj_tab = _block_pairs(n, by_query=True)

    def body(qi_ref, kj_ref, q_ref, k_ref, v_ref, o_ref, lse_ref, do_ref, dq_ref, dq_sc):
        pair = pl.program_id(1)
        qi = qi_ref[pair]
        kj = kj_ref[pair]

        @pl.when(kj == 0)
        def _():
            dq_sc[...] = jnp.zeros_like(dq_sc)

        def accumulate(diagonal):
            for rows in _row_groups(t):
                _, (ds,) = _att_probs([q_ref[rows, :]], k_ref[...], [lse_ref[rows, :]], [do_ref[rows, :]],
                                      [o_ref[rows, :]], v_ref[...], [rows], diagonal)
                dq_sc[rows, :] += _dot(ds.astype(bf16), k_ref[...], NN)

        @pl.when(kj < qi)
        def _():
            accumulate(False)

        @pl.when(kj == qi)
        def _():
            accumulate(True)
            dq_ref[...] = dq_sc[...]

    qmap = lambda h, p, qi_ref, kj_ref: (qi_ref[p], h)
    kmap = lambda h, p, qi_ref, kj_ref: (kj_ref[p], h)
    return pl.pallas_call(
        body, name="mla_att_bwd_q",
        grid_spec=pltpu.PrefetchScalarGridSpec(
            num_scalar_prefetch=2, grid=(MLA_HEADS, len(qi_tab)),
            in_specs=[pl.BlockSpec((t, QK_PAD), qmap), pl.BlockSpec((t, QK_PAD), kmap), pl.BlockSpec((t, V_HEAD), kmap),
                      pl.BlockSpec((t, V_HEAD), qmap), pl.BlockSpec((t, LANES), qmap), pl.BlockSpec((t, V_HEAD), qmap)],
            out_specs=pl.BlockSpec((t, QK_PAD), qmap),
            scratch_shapes=[pltpu.VMEM((t, QK_PAD), f32)]),
        out_shape=jax.ShapeDtypeStruct((s, MLA_HEADS * QK_PAD), f32),
        compiler_params=_params(("parallel", "arbitrary")))(qi_tab, kj_tab, q, k, v, o, lse, do)


@jax.custom_vjp
def attention(q, k, v):
    return _att_fwd(q.astype(bf16), k.astype(bf16), v.astype(bf16))[0]


def _attention_fwd(q, k, v):
    qb, kb, vb = q.astype(bf16), k.astype(bf16), v.astype(bf16)
    o, lse = _att_fwd(qb, kb, vb)
    return o, (qb, kb, vb, o, lse)


def _attention_bwd(res, do):
    qb, kb, vb, o, lse = res
    dob = do.astype(bf16)
    dk, dv = _att_bwd_kv(qb, kb, vb, o, lse, dob)
    dq = _att_bwd_q(qb, kb, vb, o, lse, dob)
    return dq, dk, dv


attention.defvjp(_attention_fwd, _attention_bwd)


def _seq_tile(s):
    return min(s, 256)


def _chan_tile(c):
    return _largest_tile(c, 512, LANES)


def _shift_down(ext, sh, t):
    if sh == 0:
        return ext[SUBLANES:]
    return pltpu.roll(ext, sh, axis=0)[SUBLANES:]


def _conv_fwd_call(x, w, b, name):
    s, c = x.shape
    t, cb = _seq_tile(s), _chan_tile(c)
    hb = t // SUBLANES

    def body(x_ref, xp_ref, w_ref, b_ref, y_ref):
        ti = pl.program_id(1)
        prev = jnp.where(ti == 0, 0.0, xp_ref[...])
        ext = jnp.concatenate([prev, x_ref[...]], axis=0)
        acc = jnp.broadcast_to(b_ref[...], (t, cb))
        for kk in range(CONV_W):
            acc = acc + w_ref[kk:kk + 1, :] * _shift_down(ext, CONV_W - 1 - kk, t)
        y_ref[...] = acc

    return pl.pallas_call(
        body, name=name + "_fwd", grid=(c // cb, s // t),
        in_specs=[pl.BlockSpec((t, cb), lambda ci, ti: (ti, ci)),
                  pl.BlockSpec((SUBLANES, cb), lambda ci, ti: (jnp.maximum(ti * hb - 1, 0), ci)),
                  pl.BlockSpec((CONV_W, cb), lambda ci, ti: (0, ci)),
                  pl.BlockSpec((1, cb), lambda ci, ti: (0, ci))],
        out_specs=pl.BlockSpec((t, cb), lambda ci, ti: (ti, ci)),
        out_shape=jax.ShapeDtypeStruct((s, c), f32),
        compiler_params=_params(("parallel", "arbitrary")))(x, x, w, b)


def _conv_bwd_call(x, w, dy, name):
    s, c = x.shape
    t, cb = _seq_tile(s), _chan_tile(c)
    hb = t // SUBLANES
    nt = s // t

    def body(x_ref, xp_ref, w_ref, dy_ref, dyn_ref, dx_ref, dw_ref, db_ref):
        ti = pl.program_id(1)

        @pl.when(ti == 0)
        def _():
            dw_ref[...] = jnp.zeros_like(dw_ref)
            db_ref[...] = jnp.zeros_like(db_ref)

        dy = dy_ref[...]
        nxt = jnp.where(ti == nt - 1, 0.0, dyn_ref[...])
        dext = jnp.concatenate([dy, nxt], axis=0)
        prev = jnp.where(ti == 0, 0.0, xp_ref[...])
        xext = jnp.concatenate([prev, x_ref[...]], axis=0)
        dx = jnp.zeros((t, cb), f32)
        for kk in range(CONV_W):
            sh = CONV_W - 1 - kk
            up = dext[:t] if sh == 0 else pltpu.roll(dext, t + SUBLANES - sh, axis=0)[:t]
            dx = dx + w_ref[kk:kk + 1, :] * up
            dw_ref[kk:kk + 1, :] += jnp.sum(dy * _shift_down(xext, sh, t), axis=0, keepdims=True)
        dx_ref[...] = dx
        db_ref[...] += jnp.sum(dy, axis=0, keepdims=True)

    return pl.pallas_call(
        body, name=name + "_bwd", grid=(c // cb, nt),
        in_specs=[pl.BlockSpec((t, cb), lambda ci, ti: (ti, ci)),
                  pl.BlockSpec((SUBLANES, cb), lambda ci, ti: (jnp.maximum(ti * hb - 1, 0), ci)),
                  pl.BlockSpec((CONV_W, cb), lambda ci, ti: (0, ci)),
                  pl.BlockSpec((t, cb), lambda ci, ti: (ti, ci)),
                  pl.BlockSpec((SUBLANES, cb), lambda ci, ti: (jnp.minimum((ti + 1) * hb, nt * hb - 1), ci))],
        out_specs=[pl.BlockSpec((t, cb), lambda ci, ti: (ti, ci)),
                   pl.BlockSpec((CONV_W, cb), lambda ci, ti: (0, ci)),
                   pl.BlockSpec((1, cb), lambda ci, ti: (0, ci))],
        out_shape=[jax.ShapeDtypeStruct((s, c), f32), jax.ShapeDtypeStruct((CONV_W, c), f32),
                   jax.ShapeDtypeStruct((1, c), f32)],
        compiler_params=_params(("parallel", "arbitrary")))(x, x, w, dy, dy)


def causal_conv(x, w, b, name):
    @jax.custom_vjp
    def op(x, w, b):
        return _conv_fwd_call(x, w, b, name)

    def fwd(x, w, b):
        return op(x, w, b), (x, w)

    def bwd(res, dy):
        x, w = res
        dx, dw, db = _conv_bwd_call(x, w, dy, name)
        return dx, dw, db

    op.defvjp(fwd, bwd)
    return op(x, w, b)


def _scan_fwd_call(a, b):
    s, c = a.shape
    t, cb = _seq_tile(s), _chan_tile(c)

    def body(a_ref, b_ref, h_ref, carry):
        ti = pl.program_id(1)

        @pl.when(ti == 0)
        def _():
            carry[...] = jnp.zeros_like(carry)

        av, bv = a_ref[...], b_ref[...]
        rows = lax.broadcasted_iota(jnp.int32, (t, cb), 0)
        d = 1
        while d < t:
            a_sh = jnp.where(rows >= d, pltpu.roll(av, d, axis=0), 1.0)
            b_sh = jnp.where(rows >= d, pltpu.roll(bv, d, axis=0), 0.0)
            bv = av * b_sh + bv
            av = av * a_sh
            d *= 2
        h = av * carry[0:1, :] + bv
        h_ref[...] = h
        carry[0:1, :] = h[t - 1:t, :]

    return pl.pallas_call(
        body, name="lru_scan_fwd", grid=(c // cb, s // t),
        in_specs=[pl.BlockSpec((t, cb), lambda ci, ti: (ti, ci))] * 2,
        out_specs=pl.BlockSpec((t, cb), lambda ci, ti: (ti, ci)),
        out_shape=jax.ShapeDtypeStruct((s, c), f32),
        scratch_shapes=[pltpu.VMEM((SUBLANES, cb), f32)],
        compiler_params=_params(("parallel", "arbitrary")))(a, b)


def _scan_bwd_call(a, h, dh):
    s, c = a.shape
    t, cb = _seq_tile(s), _chan_tile(c)
    hb = t // SUBLANES
    nt = s // t

    def body(a_ref, an_ref, h_ref, hp_ref, dh_ref, da_ref, db_ref, carry):
        step = pl.program_id(1)
        ti = nt - 1 - step

        @pl.when(step == 0)
        def _():
            carry[...] = jnp.zeros_like(carry)

        rows = lax.broadcasted_iota(jnp.int32, (t, cb), 0)
        av = a_ref[...]
        an = jnp.where(rows == t - 1, an_ref[0:1, :], pltpu.roll(av, t - 1, axis=0))
        gv = dh_ref[...]
        d = 1
        while d < t:
            a_sh = jnp.where(rows < t - d, pltpu.roll(an, t - d, axis=0), 1.0)
            g_sh = jnp.where(rows < t - d, pltpu.roll(gv, t - d, axis=0), 0.0)
            gv = an * g_sh + gv
            an = an * a_sh
            d *= 2
        g = an * carry[0:1, :] + gv
        carry[0:1, :] = g[0:1, :]
        hv = h_ref[...]
        first = jnp.where(ti == 0, 0.0, hp_ref[SUBLANES - 1:SUBLANES, :])
        h_prev = jnp.where(rows == 0, first, pltpu.roll(hv, 1, axis=0))
        da_ref[...] = g * h_prev
        db_ref[...] = g

    cur = lambda ci, st: (nt - 1 - st, ci)
    return pl.pallas_call(
        body, name="lru_scan_bwd", grid=(c // cb, nt),
        in_specs=[pl.BlockSpec((t, cb), cur),
                  pl.BlockSpec((SUBLANES, cb), lambda ci, st: (jnp.minimum((nt - st) * hb, nt * hb - 1), ci)),
                  pl.BlockSpec((t, cb), cur),
                  pl.BlockSpec((SUBLANES, cb), lambda ci, st: (jnp.maximum((nt - 1 - st) * hb - 1, 0), ci)),
                  pl.BlockSpec((t, cb), cur)],
        out_specs=[pl.BlockSpec((t, cb), cur)] * 2,
        out_shape=[jax.ShapeDtypeStruct((s, c), f32)] * 2,
        scratch_shapes=[pltpu.VMEM((SUBLANES, cb), f32)],
        compiler_params=_params(("parallel", "arbitrary")))(a, a, h, h, dh)


@jax.custom_vjp
def lru_scan(a, b):
    return _scan_fwd_call(a, b)


def _lru_scan_fwd(a, b):
    h = _scan_fwd_call(a, b)
    return h, (a, h)


def _lru_scan_bwd(res, dh):
    a, h = res
    da, db = _scan_bwd_call(a, h, dh)
    return da, db


lru_scan.defvjp(_lru_scan_fwd, _lru_scan_bwd)


def _expm1(x):
    small = x * (1.0 + x / 2.0 * (1.0 + x / 3.0 * (1.0 + x / 4.0 * (1.0 + x / 5.0 * (1.0 + x / 6.0 * (1.0 + x / 7.0))))))
    return jnp.where(jnp.abs(x) < 0.25, small, jnp.exp(x) - 1.0)


def _lru_gates(xc, w_a, b_a, w_x, b_x, lam):
    xb = xc.astype(bf16)
    r = jax.nn.sigmoid(_dot(xb, w_a.astype(bf16), NN) + b_a)
    gi = jax.nn.sigmoid(_dot(xb, w_x.astype(bf16), NN) + b_x)
    log_a = -LRU_C * r * jax.nn.softplus(-lam)
    a = jnp.exp(log_a)
    mult = jnp.sqrt(-_expm1(2.0 * log_a))
    return a, mult * (gi * xc)


def _each(fn, *lists):
    return [fn(*items) for items in zip(*lists)]


@jax.custom_vjp
def _inv_unit_lower(ls):
    c = ls[0].shape[0]
    eye = (lax.broadcasted_iota(jnp.int32, (c, c), 0) == lax.broadcasted_iota(jnp.int32, (c, c), 1)).astype(f32)
    ps = [eye - l for l in ls]
    ms = list(ls)
    span = 1
    while 2 * span < c:
        ms = [_dot(m, m, NN, MID) for m in ms]
        ps = _each(lambda p, m: p + _dot(p, m, NN, MID), ps, ms)
        span *= 2
    return tuple(ps)


def _inv_unit_lower_fwd(ls):
    ts = _inv_unit_lower(ls)
    return ts, ts


def _inv_unit_lower_bwd(ts, dts):
    inner = _each(lambda t, dt: _dot(t, dt, TN, MID), ts, dts)
    return (tuple(_each(lambda x, t: -_dot(x, t, NT, MID), inner, ts)),)


_inv_unit_lower.defvjp(_inv_unit_lower_fwd, _inv_unit_lower_bwd)


def _gdn_chunk(states, qs, ks, vs, gfull, bfull):
    c = GDN_CHUNK
    heads = list(range(len(states)))
    lane = lax.broadcasted_iota(jnp.int32, (c, LANES), 1)
    row = lax.broadcasted_iota(jnp.int32, (c, LANES), 0)
    r = lax.broadcasted_iota(jnp.int32, (c, c), 0)
    cc = lax.broadcasted_iota(jnp.int32, (c, c), 1)
    tri = (r >= cc).astype(f32)
    first_lane = (lane == 0).astype(f32)
    gs = [jnp.sum(jnp.where(lane == GDN_HEADS + h, gfull, 0.0), axis=1, keepdims=True) for h in heads]
    betas = [jnp.sum(jnp.where(lane == h, bfull, 0.0), axis=1, keepdims=True) for h in heads]
    gcs = [_dot(tri, jnp.broadcast_to(g, (c, LANES)), NN, HI) for g in gs]
    gc_cols = [_dot(first_lane, gc, NT, HI) for gc in gcs]
    gc_rows = [jnp.sum(jnp.where(lane == 0, gc, 0.0), axis=1, keepdims=True) for gc in gcs]
    decays = _each(lambda gr, gcl: jnp.exp(jnp.where(r >= cc, gr - gcl, NEG_BIG)), gc_rows, gc_cols)
    qs = [q * GDN_DK ** -0.5 for q in qs]
    k_betas = _each(lambda k, b: k * b, ks, betas)
    v_betas = _each(lambda v, b: v * b, vs, betas)
    egcs = [jnp.exp(gc) for gc in gcs]
    kkts = _each(lambda kb, k, d: _bdot(kb, k, NT) * d, k_betas, ks, decays)
    ts = _inv_unit_lower(tuple(jnp.where(r > cc, kkt, 0.0) for kkt in kkts))
    us = _each(lambda t, vb: _dot(t, vb, NN, MID), ts, v_betas)
    ws = _each(lambda t, kb, e: _dot(t, kb * e, NN, MID), ts, k_betas, egcs)
    qks = _each(lambda q, k, d: jnp.where(r >= cc, _bdot(q, k, NT) * d, 0.0), qs, ks, decays)
    gls = [jnp.sum(jnp.where(row == c - 1, gc, 0.0), axis=0, keepdims=True) for gc in gcs]
    k_tails = _each(lambda k, gl, gc: k * jnp.exp(gl - gc), ks, gls, gcs)
    v_news = _each(lambda u, w, st: u - _bdot(w, st, NN), us, ws, states)
    os_ = _each(lambda q, e, st, qk, vn: _bdot(q * e, st, NN) + _bdot(qk, vn, NN), qs, egcs, states, qks, v_news)
    new_states = _each(lambda st, gl, kt, vn: st * jnp.exp(gl) + _bdot(kt, vn, TN), states, gls, k_tails, v_news)
    return tuple(os_), tuple(new_states)


def _head_cols(h):
    return slice(h * LANES, (h + 1) * LANES)


def _gdn_fwd_call(q, k, v, gfull, bfull):
    s = q.shape[0]
    c = GDN_CHUNK
    n = s // c

    def body(q_ref, k_ref, v_ref, g_ref, b_ref, o_ref, st_ref, state):
        @pl.when(pl.program_id(0) == 0)
        def _():
            state[...] = jnp.zeros_like(state)

        heads = range(GDN_HEADS)
        states = [state[h] for h in heads]
        outs, new_states = _gdn_chunk(states, *[[ref[:, _head_cols(h)] for h in heads] for ref in (q_ref, k_ref, v_ref)],
                                      g_ref[...], b_ref[...])
        for h in heads:
            st_ref[0, h] = states[h]
            o_ref[:, _head_cols(h)] = outs[h]
            state[h] = new_states[h]

    hd = pl.BlockSpec((c, GDN_HEADS * LANES), lambda ni: (ni, 0))
    sh = pl.BlockSpec((c, LANES), lambda ni: (ni, 0))
    return pl.pallas_call(
        body, name="gdn_fwd", grid=(n,), in_specs=[hd, hd, hd, sh, sh],
        out_specs=[hd, pl.BlockSpec((1, GDN_HEADS, GDN_DK, GDN_DV), lambda ni: (ni, 0, 0, 0))],
        out_shape=[jax.ShapeDtypeStruct((s, GDN_HEADS * GDN_DV), f32),
                   jax.ShapeDtypeStruct((n, GDN_HEADS, GDN_DK, GDN_DV), f32)],
        scratch_shapes=[pltpu.VMEM((GDN_HEADS, GDN_DK, GDN_DV), f32)],
        compiler_params=_params(("arbitrary",)))(q, k, v, gfull, bfull)


def _gdn_bwd_call(q, k, v, gfull, bfull, states, do):
    s = q.shape[0]
    c = GDN_CHUNK
    n = s // c

    def body(q_ref, k_ref, v_ref, g_ref, b_ref, st_ref, do_ref, dq_ref, dk_ref, dv_ref, dg_ref, db_ref, dstate):
        @pl.when(pl.program_id(0) == 0)
        def _():
            dstate[...] = jnp.zeros_like(dstate)

        heads = range(GDN_HEADS)
        per_head = [tuple(ref[:, _head_cols(h)] for h in heads) for ref in (q_ref, k_ref, v_ref)]
        _, vjp = jax.vjp(_gdn_chunk, tuple(st_ref[0, h] for h in heads), *per_head, g_ref[...], b_ref[...])
        ds0, dq, dk, dv, dg, db = vjp((tuple(do_ref[:, _head_cols(h)] for h in heads), tuple(dstate[h] for h in heads)))
        for h in heads:
            dstate[h] = ds0[h]
            dq_ref[:, _head_cols(h)] = dq[h]
            dk_ref[:, _head_cols(h)] = dk[h]
            dv_ref[:, _head_cols(h)] = dv[h]
        dg_ref[...] = dg
        db_ref[...] = db

    hd = pl.BlockSpec((c, GDN_HEADS * LANES), lambda st: (n - 1 - st, 0))
    sh = pl.BlockSpec((c, LANES), lambda st: (n - 1 - st, 0))
    big = jax.ShapeDtypeStruct((s, GDN_HEADS * GDN_DV), f32)
    small = jax.ShapeDtypeStruct((s, LANES), f32)
    return pl.pallas_call(
        body, name="gdn_bwd", grid=(n,),
        in_specs=[hd, hd, hd, sh, sh,
                  pl.BlockSpec((1, GDN_HEADS, GDN_DK, GDN_DV), lambda st: (n - 1 - st, 0, 0, 0)), hd],
        out_specs=[hd, hd, hd, sh, sh], out_shape=[big, big, big, small, small],
        scratch_shapes=[pltpu.VMEM((GDN_HEADS, GDN_DK, GDN_DV), f32)],
        compiler_params=_params(("arbitrary",)))(q, k, v, gfull, bfull, states, do)


@jax.custom_vjp
def gdn_core(q, k, v, gfull, bfull):
    return _gdn_fwd_call(q, k, v, gfull, bfull)[0]


def _gdn_core_fwd(q, k, v, gfull, bfull):
    o, states = _gdn_fwd_call(q, k, v, gfull, bfull)
    return o, (q, k, v, gfull, bfull, states)


def _gdn_core_bwd(res, do):
    return tuple(_gdn_bwd_call(*res, do))


gdn_core.defvjp(_gdn_core_fwd, _gdn_core_bwd)


def _l2norm(t):
    return t * lax.rsqrt(jnp.sum(t * t, axis=-1, keepdims=True) + EPS)


def _gdn_pre(qc, kc, vc):
    return _l2norm(jax.nn.silu(qc)), _l2norm(jax.nn.silu(kc)), jax.nn.silu(vc)


def _gdn_gates(gba, a_log, dt_bias):
    beta = jax.nn.sigmoid(gba)
    g = -jnp.exp(a_log) * jax.nn.softplus(gba + dt_bias)
    return beta, g


def _gdn_post(o, z, w):
    return (_rms(o, w) * jax.nn.silu(z),)


def _merge(u0, u1, u2, g0, g1, g2, b0, b1, b2):
    return (jax.nn.sigmoid(g0 + b0) * u0 + jax.nn.sigmoid(g1 + b1) * u1 + jax.nn.sigmoid(g2 + b2) * u2,)


def _rope_q(qa, qb, ctab, stab):
    return (qa * ctab + qb * stab,)


def _rope_k(kp, kra, krb, ctab, stab):
    return (kp + kra * ctab + krb * stab,)


def _swap_rot(w):
    half = QK_ROPE // 2
    return jnp.concatenate([-w[..., half:], w[..., :half]], axis=-1)


def _ext_w_in(w):
    d = w.shape[0]
    z = lambda n: jnp.zeros((d, n), w.dtype)
    kr = w[:, 1024:1088]
    return jnp.concatenate([
        w[:, :1024], z(QK_NOPE), kr, z(64), z(QK_NOPE), _swap_rot(kr), z(64),
        w[:, 1088:7232], w[:, 7232:7248], z(112), w[:, 7248:], z(128)], axis=1)


def _ext_w_q(w):
    d = w.shape[0]
    w3 = w.reshape(d, MLA_HEADS, QK_NOPE + QK_ROPE)
    nope, pe = w3[..., :QK_NOPE], w3[..., QK_NOPE:]
    z64 = jnp.zeros((d, MLA_HEADS, 64), w.dtype)
    z128 = jnp.zeros((d, MLA_HEADS, QK_NOPE), w.dtype)
    a = jnp.concatenate([nope, pe, z64], axis=-1).reshape(d, MLA_HEADS * QK_PAD)
    b = jnp.concatenate([z128, _swap_rot(pe), z64], axis=-1).reshape(d, MLA_HEADS * QK_PAD)
    return jnp.concatenate([a, b], axis=1)


def _ext_w_kv(w):
    d = w.shape[0]
    w3 = w.reshape(d, MLA_HEADS, QK_NOPE + V_HEAD)
    kn, v = w3[..., :QK_NOPE], w3[..., QK_NOPE:]
    kp = jnp.concatenate([kn, jnp.zeros_like(kn)], axis=-1).reshape(d, MLA_HEADS * QK_PAD)
    return jnp.concatenate([kp, v.reshape(d, MLA_HEADS * V_HEAD)], axis=1)


def _lane_pad(vec8):
    return jnp.concatenate([jnp.zeros((8,), f32), vec8, jnp.zeros((LANES - 16,), f32)]).reshape(1, LANES)


def layer(x, p, ctab, stab, li):
    tag = f"l{li}_"
    h = rms_op(x, p["norm_mix"], tag + "norm_mix")
    proj = linear(h, _ext_w_in(p["w_in"]), tag + "w_in")
    (c_q, c_kv, kra, krb, lru_x, lru_y, g_q, g_k, g_v, g_z, g_ba, gl0, gl1, gl2, _) = split_cols(
        proj, [w for _, w in IN_PIECES])

    cqn = rms_op(c_q, p["mla_q_norm"], tag + "q_norm")
    ckvn = rms_op(c_kv, p["mla_kv_norm"], tag + "kv_norm")
    qall = linear(cqn, _ext_w_q(p["mla_w_uq"]), tag + "w_uq")
    kvall = linear(ckvn, _ext_w_kv(p["mla_w_ukv"]), tag + "w_ukv")
    qa, qb = split_cols(qall, [MLA_HEADS * QK_PAD] * 2)
    kp, v = split_cols(kvall, [MLA_HEADS * QK_PAD, MLA_HEADS * V_HEAD])
    q = row_op(tag + "rope_q", _rope_q, [(qa, "g"), (qb, "g"), (ctab, "sn"), (stab, "sn")], [], [(QK_PAD, f32)],
               MLA_HEADS)[0]
    k = row_op(tag + "rope_k", _rope_k, [(kp, "g"), (kra, "s"), (krb, "s"), (ctab, "sn"), (stab, "sn")], [],
               [(QK_PAD, f32)], MLA_HEADS)[0]
    y_mla = attention(q, k, v)

    xc = causal_conv(lru_x, p["lru_conv_w"], p["lru_conv_b"].reshape(1, -1), tag + "lru_conv")
    gshape = (LRU_BLOCKS, 1, LRU_BLOCK_W)
    a, bx = row_op(tag + "lru_gates", _lru_gates, [(xc, "g")],
                   [(p["lru_w_a"], "p"), (p["lru_b_a"].reshape(gshape), "p"), (p["lru_w_x"], "p"),
                    (p["lru_b_x"].reshape(gshape), "p"), (p["lru_lambda"].reshape(gshape), "p")],
                   [(LRU_BLOCK_W, f32)] * 2, LRU_BLOCKS)
    hs = lru_scan(a, bx)
    y_lru = row_op(tag + "lru_out", lambda hh, yy: (hh * jax.nn.gelu(yy),), [(hs, "g"), (lru_y, "g")], [],
                   [(LRU_WIDTH, f32)])[0]

    cw = p["gdn_conv_w"]
    nob = jnp.zeros((1, GDN_HEADS * GDN_DK), f32)
    qc = causal_conv(g_q, cw[:, :1024], nob, tag + "gdn_conv_q")
    kc = causal_conv(g_k, cw[:, 1024:2048], nob, tag + "gdn_conv_k")
    vc = causal_conv(g_v, cw[:, 2048:], nob, tag + "gdn_conv_v")
    qn, kn, vs = row_op(tag + "gdn_pre", _gdn_pre, [(qc, "g"), (kc, "g"), (vc, "g")], [], [(GDN_DK, f32)] * 3,
                        GDN_HEADS)
    bfull, gfull = row_op(tag + "gdn_gates", _gdn_gates, [(g_ba, "g")],
                          [(_lane_pad(p["gdn_a_log"]), "c"), (_lane_pad(p["gdn_dt_bias"]), "c")], [(LANES, f32)] * 2)
    o = gdn_core(qn, kn, vs, gfull, bfull)
    y_gdn = row_op(tag + "gdn_post", _gdn_post, [(o, "g"), (g_z, "g")], [(p["gdn_norm"].reshape(1, -1), "c")],
                   [(GDN_DV, f32)], GDN_HEADS)[0]

    wb = p["w_branch"]
    u0 = linear(y_mla, wb[0], tag + "w_branch0")
    u1 = linear(y_lru, wb[1], tag + "w_branch1")
    u2 = linear(y_gdn, wb[2], tag + "w_branch2")
    mixed = row_op(tag + "merge", _merge, [(u0, "g"), (u1, "g"), (u2, "g"), (gl0, "g"), (gl1, "g"), (gl2, "g")],
                   [(p["b_gate"][nb:nb + 1], "c") for nb in range(N_BRANCH)], [(D_MODEL, f32)])[0]
    x = x + linear(mixed, p["w_out"], tag + "w_out")

    h2 = rms_op(x, p["norm_ffn"], tag + "norm_ffn")
    gt = linear(h2, p["ffn_w_gate"], tag + "ffn_gate")
    up = linear(h2, p["ffn_w_up"], tag + "ffn_up")
    act = row_op(tag + "swiglu", lambda a_, b_: (jax.nn.silu(a_) * b_,), [(gt, "g"), (up, "g")], [],
                 [(FFN_HIDDEN, f32)])[0]
    return x + linear(act, p["ffn_w_down"], tag + "ffn_down")


def trunk(layers, x, ctab, stab):
    for li, p in enumerate(layers):
        x = layer(x, p, ctab, stab, li)
    return x


def loss_head(x, target, w):
    s, d = x.shape
    tile = _row_tile(s, [d] * 3)

    def fn(xv, wv, tv):
        err = jnp.square(_rms(xv, wv) - tv)
        return 0.5 * jnp.sum(jnp.mean(err, axis=-1, keepdims=True), axis=0, keepdims=True)

    def body(x_ref, t_ref, w_ref, loss_ref, dx_ref, dw_ref):
        @pl.when(pl.program_id(0) == 0)
        def _():
            loss_ref[...] = jnp.zeros_like(loss_ref)
            dw_ref[...] = jnp.zeros_like(dw_ref)

        tv = t_ref[...]
        val, vjp = jax.vjp(lambda xv, wv: fn(xv, wv, tv), x_ref[...], w_ref[...])
        dx, dw = vjp(jnp.ones((1, 1), f32))
        loss_ref[...] += jnp.broadcast_to(val, loss_ref.shape)
        dx_ref[...] = dx
        dw_ref[...] += dw

    loss, dx, dw = pl.pallas_call(
        body, name="loss_head", grid=(s // tile,),
        in_specs=[pl.BlockSpec((tile, d), lambda i: (i, 0)), pl.BlockSpec((tile, d), lambda i: (i, 0)),
                  pl.BlockSpec((1, d), lambda i: (0, 0))],
        out_specs=[pl.BlockSpec((SUBLANES, LANES), lambda i: (0, 0)), pl.BlockSpec((tile, d), lambda i: (i, 0)),
                   pl.BlockSpec((1, d), lambda i: (0, 0))],
        out_shape=[jax.ShapeDtypeStruct((SUBLANES, LANES), f32), jax.ShapeDtypeStruct((s, d), f32),
                   jax.ShapeDtypeStruct((1, d), f32)],
        compiler_params=_params(("arbitrary",)))(x, target, w.reshape(1, d))
    return loss[0, 0], dx, dw.reshape(d)


def local_step(layers, norm_final, x, positions, target):
    ctab, stab = rope_tables(positions.reshape(-1, 1))
    y, pull = jax.vjp(lambda ls, xx: trunk(ls, xx, ctab, stab), layers, x)
    loss, dy, d_final = loss_head(y, target, norm_final)
    d_layers, dx = pull(dy)
    return loss, dx, d_layers, d_final


def _flat2d(a):
    return a.reshape(-1, a.shape[-1])


def _ew_tile(rows, cols, n_arrays):
    budget = 16 * 1024 * 1024
    cap = max(SUBLANES, budget // (2 * 4 * cols * n_arrays))
    if rows <= cap:
        return rows
    return _largest_tile(rows, cap, SUBLANES)


def elementwise(name, fn, arrays, out_dtypes):
    shape = arrays[0].shape
    flat = [_flat2d(a) for a in arrays]
    rows, cols = flat[0].shape
    tile = _ew_tile(rows, cols, len(arrays) + len(out_dtypes))
    n_in = len(arrays)

    def body(*refs):
        res = fn(*[r[...] for r in refs[:n_in]])
        for o_ref, o in zip(refs[n_in:], res):
            o_ref[...] = o.astype(o_ref.dtype)

    spec = pl.BlockSpec((tile, cols), lambda i: (i, 0))
    res = pl.pallas_call(
        body, name=name, grid=(rows // tile,), in_specs=[spec] * n_in, out_specs=[spec] * len(out_dtypes),
        out_shape=[jax.ShapeDtypeStruct((rows, cols), dt) for dt in out_dtypes],
        compiler_params=_params(("parallel",)))(*flat)
    return [r.reshape(shape) for r in res]


def _adamw(w, g, m, v):
    m = ADAM_B1 * m + (1.0 - ADAM_B1) * g
    v = ADAM_B2 * v + (1.0 - ADAM_B2) * jnp.square(g)
    m_hat = m / (1.0 - ADAM_B1 ** ADAM_STEP)
    v_hat = v / (1.0 - ADAM_B2 ** ADAM_STEP)
    delta = -ADAM_LR * (m_hat / (jnp.sqrt(v_hat) + ADAM_EPS) + ADAM_WD * w)
    return delta, m, v


def _place():
    x, y, c = lax.axis_index("x"), lax.axis_index("y"), lax.axis_index("c")
    return x, y, c, 2 * x + y


CHIP_FLIPS = ((1, 0), (0, 1), (1, 1))


def _hbm_specs(n):
    return [pl.BlockSpec(memory_space=pl.ANY)] * n


def gather_weights(shards, name):
    n = len(shards)

    def body(*refs):
        src = refs[:n]
        out = refs[n:2 * n]
        send1, recv1, send2, recv2, loc = refs[2 * n:]
        x, y, c, j = _place()
        mine = pl.ds(c * HALF_LAYERS, HALF_LAYERS)
        other = pl.ds((1 - c) * HALF_LAYERS, HALF_LAYERS)
        locals_ = []
        for a in range(n):
            for hh, sl in ((c, mine), (1 - c, other)):
                cp = pltpu.make_async_copy(src[a].at[sl], out[a].at[hh, j], loc.at[2 * a + (0 if sl is mine else 1)])
                cp.start()
                locals_.append(cp)
        firsts = []
        for a in range(n):
            for kk, (fx, fy) in enumerate(CHIP_FLIPS):
                cp = pltpu.make_async_remote_copy(
                    src_ref=src[a].at[mine], dst_ref=out[a].at[c, j], send_sem=send1.at[3 * a + kk],
                    recv_sem=recv1.at[3 * a + kk], device_id=(x ^ fx, y ^ fy, c), device_id_type=MESH)
                cp.start()
                firsts.append(cp)
        passed = []
        for a in range(n):
            for kk, (fx, fy) in enumerate(CHIP_FLIPS):
                jp = j ^ (2 * fx + fy)
                landed = out[a].at[c, jp]
                pltpu.make_async_remote_copy(
                    src_ref=landed, dst_ref=landed, send_sem=send1.at[3 * a + kk], recv_sem=recv1.at[3 * a + kk],
                    device_id=(x ^ fx, y ^ fy, c), device_id_type=MESH).wait_recv()
                cp = pltpu.make_async_remote_copy(
                    src_ref=landed, dst_ref=landed, send_sem=send2.at[3 * a + kk], recv_sem=recv2.at[3 * a + kk],
                    device_id=(x, y, 1 - c), device_id_type=MESH)
                cp.start()
                passed.append(cp)
        for a in range(n):
            for kk, (fx, fy) in enumerate(CHIP_FLIPS):
                jp = j ^ (2 * fx + fy)
                theirs = out[a].at[1 - c, jp]
                pltpu.make_async_remote_copy(
                    src_ref=theirs, dst_ref=theirs, send_sem=send2.at[3 * a + kk], recv_sem=recv2.at[3 * a + kk],
                    device_id=(x, y, 1 - c), device_id_type=MESH).wait_recv()
        for cp in firsts + passed:
            cp.wait_send()
        for cp in locals_:
            cp.wait()

    out_shape = [jax.ShapeDtypeStruct((2, N_CHIPS, HALF_LAYERS) + s.shape[1:], s.dtype) for s in shards]
    return pl.pallas_call(
        body, name=name, in_specs=_hbm_specs(n), out_specs=_hbm_specs(n), out_shape=out_shape,
        scratch_shapes=[pltpu.SemaphoreType.DMA((3 * n,)), pltpu.SemaphoreType.DMA((3 * n,)),
                        pltpu.SemaphoreType.DMA((3 * n,)), pltpu.SemaphoreType.DMA((3 * n,)),
                        pltpu.SemaphoreType.DMA((2 * n,))])(*shards)


def swap_with_sibling(arrays, name, lead_other_half=False):
    n = len(arrays)

    def body(*refs):
        src = refs[:n]
        out = refs[n:2 * n]
        send, recv = refs[2 * n:]
        x, y, c, _ = _place()
        cps = []
        for a in range(n):
            s_ref = src[a].at[:, pl.ds((1 - c) * HALF_LAYERS, HALF_LAYERS)] if lead_other_half else src[a]
            cp = pltpu.make_async_remote_copy(src_ref=s_ref, dst_ref=out[a], send_sem=send.at[a], recv_sem=recv.at[a],
                                              device_id=(x, y, 1 - c), device_id_type=MESH)
            cp.start()
            cps.append(cp)
        for cp in cps:
            cp.wait()

    if lead_other_half:
        out_shape = [jax.ShapeDtypeStruct((N_CHIPS, HALF_LAYERS) + a.shape[2:], a.dtype) for a in arrays]
    else:
        out_shape = [jax.ShapeDtypeStruct(a.shape, a.dtype) for a in arrays]
    return pl.pallas_call(
        body, name=name, in_specs=_hbm_specs(n), out_specs=_hbm_specs(n), out_shape=out_shape,
        scratch_shapes=[pltpu.SemaphoreType.DMA((n,)), pltpu.SemaphoreType.DMA((n,))])(*arrays)


def scatter_to_chips(arrays, name):
    n = len(arrays)

    def body(*refs):
        src = refs[:n]
        out = refs[n:2 * n]
        send, recv = refs[2 * n:]
        x, y, c, j = _place()
        cps = []
        for a in range(n):
            for kk, (fx, fy) in enumerate(CHIP_FLIPS):
                cp = pltpu.make_async_remote_copy(
                    src_ref=src[a].at[j ^ (2 * fx + fy)], dst_ref=out[a].at[kk], send_sem=send.at[3 * a + kk],
                    recv_sem=recv.at[3 * a + kk], device_id=(x ^ fx, y ^ fy, c), device_id_type=MESH)
                cp.start()
                cps.append(cp)
        for cp in cps:
            cp.wait()

    out_shape = [jax.ShapeDtypeStruct((3,) + a.shape[1:], a.dtype) for a in arrays]
    return pl.pallas_call(
        body, name=name, in_specs=_hbm_specs(n), out_specs=_hbm_specs(n), out_shape=out_shape,
        scratch_shapes=[pltpu.SemaphoreType.DMA((3 * n,)), pltpu.SemaphoreType.DMA((3 * n,))])(*arrays)


def share_halves(arrays, name):
    n = len(arrays)

    def body(*refs):
        src = refs[:n]
        out = refs[n:2 * n]
        send, recv, loc = refs[2 * n:]
        x, y, c, _ = _place()
        cps, lcs = [], []
        for a in range(n):
            lc = pltpu.make_async_copy(src[a], out[a].at[c], loc.at[a])
            lc.start()
            lcs.append(lc)
            cp = pltpu.make_async_remote_copy(src_ref=src[a], dst_ref=out[a].at[c], send_sem=send.at[a],
                                              recv_sem=recv.at[a], device_id=(x, y, 1 - c), device_id_type=MESH)
            cp.start()
            cps.append(cp)
        for a in range(n):
            cps[a].wait_send()
            theirs = out[a].at[1 - c]
            pltpu.make_async_remote_copy(src_ref=theirs, dst_ref=theirs, send_sem=send.at[a], recv_sem=recv.at[a],
                                         device_id=(x, y, 1 - c), device_id_type=MESH).wait_recv()
            lcs[a].wait()

    out_shape = [jax.ShapeDtypeStruct((2,) + a.shape, a.dtype) for a in arrays]
    return pl.pallas_call(
        body, name=name, in_specs=_hbm_specs(n), out_specs=_hbm_specs(n), out_shape=out_shape,
        scratch_shapes=[pltpu.SemaphoreType.DMA((n,)), pltpu.SemaphoreType.DMA((n,)),
                        pltpu.SemaphoreType.DMA((n,))])(*arrays)


def gather_all_devices(vec, name):
    def body(src, out, send, recv, loc):
        x, y, c, _ = _place()
        me = 4 * x + 2 * y + c
        lc = pltpu.make_async_copy(src, out.at[me], loc.at[0])
        lc.start()
        cps = []
        for mask in range(1, 8):
            fx, fy, fc = (mask >> 2) & 1, (mask >> 1) & 1, mask & 1
            cp = pltpu.make_async_remote_copy(src_ref=src, dst_ref=out.at[me], send_sem=send.at[mask - 1],
                                              recv_sem=recv.at[mask - 1], device_id=(x ^ fx, y ^ fy, c ^ fc),
                                              device_id_type=MESH)
            cp.start()
            cps.append(cp)
        for mask in range(1, 8):
            cps[mask - 1].wait_send()
            theirs = out.at[me ^ mask]
            pltpu.make_async_remote_copy(src_ref=theirs, dst_ref=theirs, send_sem=send.at[mask - 1],
                                         recv_sem=recv.at[mask - 1], device_id=(x, y, c), device_id_type=MESH).wait_recv()
        lc.wait()

    return pl.pallas_call(
        body, name=name, in_specs=_hbm_specs(1), out_specs=pl.BlockSpec(memory_space=pl.ANY),
        out_shape=jax.ShapeDtypeStruct((8,) + vec.shape, vec.dtype),
        scratch_shapes=[pltpu.SemaphoreType.DMA((7,)), pltpu.SemaphoreType.DMA((7,)), pltpu.SemaphoreType.DMA((1,))])(vec)


def _to_chip_major(g, axis):
    shp = g.shape
    g = g.reshape(shp[:axis] + (N_CHIPS, shp[axis] // N_CHIPS) + shp[axis + 1:])
    return jnp.moveaxis(g, axis, 0)


def _from_chip_major(b, axis):
    b = jnp.moveaxis(b, 0, axis)
    shp = b.shape
    return b.reshape(shp[:axis] + (shp[axis] * shp[axis + 1],) + shp[axis + 2:])


def _pack(arrays):
    flat = jnp.concatenate([a.reshape(-1) for a in arrays])
    pad = (-flat.shape[0]) % (256 * LANES)
    return jnp.concatenate([flat, jnp.zeros((pad,), flat.dtype)]).reshape(-1, LANES)


def _unpack(packed, like):
    flat = packed.reshape(-1)
    out, off = [], 0
    for a in like:
        out.append(flat[off:off + a.size].reshape(a.shape))
        off += a.size
    return out


def kernel(x, positions, norm_mix, w_in, mla_q_norm, mla_w_uq, mla_kv_norm, mla_w_ukv, lru_conv_w, lru_conv_b, lru_w_a, lru_b_a, lru_w_x, lru_b_x, lru_lambda, gdn_conv_w, gdn_a_log, gdn_dt_bias, gdn_norm, w_branch, b_gate, w_out, norm_ffn, ffn_w_gate, ffn_w_up, ffn_w_down, norm_final, loss_target, m_norm_mix, m_w_in, m_mla_q_norm, m_mla_w_uq, m_mla_kv_norm, m_mla_w_ukv, m_lru_conv_w, m_lru_conv_b, m_lru_w_a, m_lru_b_a, m_lru_w_x, m_lru_b_x, m_lru_lambda, m_gdn_conv_w, m_gdn_a_log, m_gdn_dt_bias, m_gdn_norm, m_w_branch, m_b_gate, m_w_out, m_norm_ffn, m_ffn_w_gate, m_ffn_w_up, m_ffn_w_down, m_norm_final, v_norm_mix, v_w_in, v_mla_q_norm, v_mla_w_uq, v_mla_kv_norm, v_mla_w_ukv, v_lru_conv_w, v_lru_conv_b, v_lru_w_a, v_lru_b_a, v_lru_w_x, v_lru_b_x, v_lru_lambda, v_gdn_conv_w, v_gdn_a_log, v_gdn_dt_bias, v_gdn_norm, v_w_branch, v_b_gate, v_w_out, v_norm_ffn, v_ffn_w_gate, v_ffn_w_up, v_ffn_w_down, v_norm_final):
    given = dict(locals())
    w = {n: given[n] for n in WEIGHTS}
    m = {n: given["m_" + n] for n in WEIGHTS}
    v = {n: given["v_" + n] for n in WEIGHTS}

    wire = {n: (w[n] if n in WIRE_F32 else elementwise("cast_" + n, lambda a: (a,), [w[n]], [bf16])[0]) for n in SHARDED}
    big = [n for n in SHARDED if n not in WIRE_F32]
    gathered = dict(zip(big, gather_weights([wire[n] for n in big], "gather_big")))
    gathered.update(zip(WIRE_F32, gather_weights([wire[n] for n in WIRE_F32], "gather_small")))
    layers = []
    for li in range(DEPTH):
        p = {n: _from_chip_major(gathered[n][li // HALF_LAYERS, :, li % HALF_LAYERS], SHARD_AXIS[n]) for n in SHARDED}
        p.update({n: w[n][li] for n in REPLICATED if n != "norm_final"})
        layers.append(p)

    loss, grad_x, d_layers, d_final = local_step(layers, norm_final, x[0], positions[0], loss_target[0])
    loss = lax.psum(loss, ("x", "y", "c"))

    grads = {}
    chip_major = {n: jnp.stack([_to_chip_major(d_layers[li][n], SHARD_AXIS[n]) for li in range(DEPTH)], axis=1)
                  for n in SHARDED}
    c = lax.axis_index("c")
    chip = 2 * lax.axis_index("x") + lax.axis_index("y")
    for group, names in (("big", big), ("small", list(WIRE_F32))):
        theirs = swap_with_sibling([chip_major[n] for n in names], "grad_pair_" + group, lead_other_half=True)
        pair16, pair32 = [], []
        for n, t in zip(names, theirs):
            mine = lax.dynamic_slice_in_dim(chip_major[n], c * HALF_LAYERS, HALF_LAYERS, axis=1)
            s16, s32 = elementwise("pair_sum_" + n, lambda a, b: (a.astype(f32) + b.astype(f32),) * 2, [mine, t],
                                   [mine.dtype, f32])
            pair16.append(s16)
            pair32.append(s32)
        landed = scatter_to_chips(pair16, "grad_chips_" + group)
        halves = []
        for n, s32, l in zip(names, pair32, landed):
            own = lax.dynamic_index_in_dim(s32, chip, axis=0, keepdims=False)
            halves.append(elementwise(
                "chip_sum_" + n, lambda o, a, b, d: (o + a.astype(f32) + b.astype(f32) + d.astype(f32),),
                [own, l[0], l[1], l[2]], [f32])[0])
        for n, both in zip(names, share_halves(halves, "grad_share_" + group)):
            grads[n] = both.reshape(w[n].shape)

    rep_g = [jnp.stack([d_layers[li][n] for li in range(DEPTH)]) for n in REPLICATED if n != "norm_final"] + [d_final]
    everyone = gather_all_devices(_pack(rep_g), "grad_replicated")
    total = elementwise("replicated_sum", lambda *a: (functools.reduce(lambda p, q: p + q, a),),
                        [everyone[d] for d in range(8)], [f32])[0]
    grads.update(zip(REPLICATED, _unpack(total, [w[n] for n in REPLICATED])))

    delta, new_m, new_v = {}, {}, {}
    for n in SHARDED:
        delta[n], new_m[n], new_v[n] = elementwise("adamw_" + n, _adamw, [w[n], grads[n], m[n], v[n]], [f32] * 3)
    rep = [n for n in REPLICATED]
    pd, pm, pv = elementwise("adamw_replicated", _adamw,
                             [_pack([w[n] for n in rep]), total, _pack([m[n] for n in rep]), _pack([v[n] for n in rep])],
                             [f32] * 3)
    for dst, packed in ((delta, pd), (new_m, pm), (new_v, pv)):
        dst.update(zip(rep, _unpack(packed, [w[n] for n in rep])))

    return (loss, grad_x[None], *[grads[n] for n in WEIGHTS], *[delta[n] for n in WEIGHTS],
            *[new_m[n] for n in WEIGHTS], *[new_v[n] for n in WEIGHTS])
```

```python
import functools
import math

import jax
import jax.numpy as jnp
from jax import lax
from jax.experimental import pallas as pl
from jax.experimental.pallas import tpu as pltpu

f32 = jnp.float32
bf16 = jnp.bfloat16
MESH = pl.DeviceIdType.MESH

D_MODEL = 2048
DEPTH = 4
EPS = 1e-6
N_BRANCH = 3
BRANCH_WIDTH = 1024
MLA_HEADS = 8
QK_NOPE = 128
QK_ROPE = 64
V_HEAD = 128
Q_LORA = 512
KV_LORA = 512
ROPE_THETA = 10000.0
LRU_WIDTH = 1024
LRU_BLOCKS = 8
LRU_BLOCK_W = 128
LRU_C = 8.0
CONV_W = 4
GDN_HEADS = 8
GDN_DK = 128
GDN_DV = 128
GDN_CHUNK = 64
FFN_HIDDEN = 5632
QK_PAD = 256
ATT_SCALE = (QK_NOPE + QK_ROPE) ** -0.5

ADAM_LR = 0.001
ADAM_B1 = 0.9
ADAM_B2 = 0.999
ADAM_EPS = 1e-08
ADAM_WD = 0.01
ADAM_STEP = 10

N_CHIPS = 4
HALF_LAYERS = DEPTH // 2
LANES = 128
SUBLANES = 8
NEG_BIG = -1e30

SHARDED = ["w_in", "mla_w_uq", "mla_w_ukv", "lru_conv_w", "gdn_conv_w", "w_branch", "b_gate", "w_out",
           "ffn_w_gate", "ffn_w_up", "ffn_w_down"]
SHARD_AXIS = {"w_in": 1, "mla_w_uq": 1, "mla_w_ukv": 1, "lru_conv_w": 1, "gdn_conv_w": 1, "w_branch": 2,
              "b_gate": 1, "w_out": 0, "ffn_w_gate": 1, "ffn_w_up": 1, "ffn_w_down": 0}
WIRE_F32 = ("lru_conv_w", "gdn_conv_w", "b_gate")
REPLICATED = ["norm_mix", "mla_q_norm", "mla_kv_norm", "lru_conv_b", "lru_w_a", "lru_b_a", "lru_w_x", "lru_b_x",
              "lru_lambda", "gdn_a_log", "gdn_dt_bias", "gdn_norm", "norm_ffn", "norm_final"]
WEIGHTS = ["norm_mix", "w_in", "mla_q_norm", "mla_w_uq", "mla_kv_norm", "mla_w_ukv", "lru_conv_w", "lru_conv_b",
           "lru_w_a", "lru_b_a", "lru_w_x", "lru_b_x", "lru_lambda", "gdn_conv_w", "gdn_a_log", "gdn_dt_bias",
           "gdn_norm", "w_branch", "b_gate", "w_out", "norm_ffn", "ffn_w_gate", "ffn_w_up", "ffn_w_down",
           "norm_final"]

IN_PIECES = [("c_q", 512), ("c_kv", 512), ("kra", QK_PAD), ("krb", QK_PAD), ("lru_x", 1024), ("lru_y", 1024),
             ("g_q", 1024), ("g_k", 1024), ("g_v", 1024), ("g_z", 1024), ("g_ba", 128),
             ("gl0", D_MODEL), ("gl1", D_MODEL), ("gl2", D_MODEL), ("pad", 128)]
IN_EXT = sum(w for _, w in IN_PIECES)


def _dot(a, b, dims, precision=None):
    return lax.dot_general(a, b, (dims, ((), ())), precision=precision, preferred_element_type=f32)


NN = ((1,), (0,))
NT = ((1,), (1,))
TN = ((0,), (0,))
HI = lax.Precision.HIGHEST
MID = lax.Precision.HIGH


def _bdot(a, b, dims):
    return _dot(a.astype(bf16), b.astype(bf16), dims)


def _largest_tile(n, cap, quantum):
    if n <= cap:
        return n
    best = None
    for t in range(quantum, cap + 1, quantum):
        if n % t == 0:
            best = t
    assert best is not None, (n, cap, quantum)
    return best


def _params(sem):
    return pltpu.CompilerParams(dimension_semantics=sem)


def _matmul(a, b, *, dims, out_dtype, name):
    if dims == TN:
        kdim, m = a.shape
        kdim2, n = b.shape
    elif dims == NT:
        m, kdim = a.shape
        n, kdim2 = b.shape
    else:
        m, kdim = a.shape
        kdim2, n = b.shape
    assert kdim == kdim2, (a.shape, b.shape, dims)
    tm = _largest_tile(m, 1024, LANES)
    tn = _largest_tile(n, 1536, LANES)
    tk = _largest_tile(kdim, 1536, LANES)
    nk = kdim // tk

    def body(a_ref, b_ref, o_ref, acc_ref):
        k = pl.program_id(2)

        @pl.when(k == 0)
        def _():
            acc_ref[...] = jnp.zeros_like(acc_ref)

        acc_ref[...] += _dot(a_ref[...].astype(bf16), b_ref[...].astype(bf16), dims)

        @pl.when(k == nk - 1)
        def _():
            o_ref[...] = acc_ref[...].astype(o_ref.dtype)

    if dims == TN:
        a_spec = pl.BlockSpec((tk, tm), lambda i, j, k: (k, i))
        b_spec = pl.BlockSpec((tk, tn), lambda i, j, k: (k, j))
    elif dims == NT:
        a_spec = pl.BlockSpec((tm, tk), lambda i, j, k: (i, k))
        b_spec = pl.BlockSpec((tn, tk), lambda i, j, k: (j, k))
    else:
        a_spec = pl.BlockSpec((tm, tk), lambda i, j, k: (i, k))
        b_spec = pl.BlockSpec((tk, tn), lambda i, j, k: (k, j))
    return pl.pallas_call(
        body, name=name, grid=(m // tm, n // tn, nk), in_specs=[a_spec, b_spec],
        out_specs=pl.BlockSpec((tm, tn), lambda i, j, k: (i, j)),
        out_shape=jax.ShapeDtypeStruct((m, n), out_dtype),
        scratch_shapes=[pltpu.VMEM((tm, tn), f32)],
        compiler_params=_params(("parallel", "parallel", "arbitrary")))(a, b)


def linear(a, w, name):
    @jax.custom_vjp
    def op(a, w):
        return _matmul(a, w, dims=NN, out_dtype=f32, name=name + "_fwd")

    def fwd(a, w):
        return op(a, w), (a, w)

    def bwd(res, dy):
        a, w = res
        da = _matmul(dy, w, dims=NT, out_dtype=f32, name=name + "_da")
        dw = _matmul(a, dy, dims=TN, out_dtype=w.dtype, name=name + "_dw")
        return da, dw

    op.defvjp(fwd, bwd)
    return op(a, w)


def _row_tile(s, widths):
    budget = 12 * 1024 * 1024
    t = 512
    while t > SUBLANES and 2 * 4 * t * sum(widths) > budget:
        t //= 2
    return min(t, s)


def row_op(name, fn, rows, consts, outs, groups=1):
    s = rows[0][0].shape[0]
    r_arrs = [a for a, _ in rows]
    r_kinds = [k for _, k in rows]
    c_arrs = [a for a, _ in consts]
    c_kinds = [k for _, k in consts]
    r_w = [a.shape[1] // groups if k[0] == "g" else a.shape[1] for a, k in rows]
    o_w = [w for w, _ in outs]
    n_r, n_c, n_o = len(rows), len(consts), len(outs)
    diff_r = [i for i, k in enumerate(r_kinds) if not k.endswith("n")]

    def r_spec(i, tile):
        if r_kinds[i][0] == "g":
            return pl.BlockSpec((tile, r_w[i]), lambda r, g: (r, g))
        return pl.BlockSpec((tile, r_w[i]), lambda r, g: (r, 0))

    def c_spec(i):
        nd = c_arrs[i].ndim
        return pl.BlockSpec(c_arrs[i].shape, lambda r, g, nd=nd: (0,) * nd)

    def o_spec(i, tile):
        return pl.BlockSpec((tile, o_w[i]), lambda r, g: (r, g))

    def c_val(ref, kind, g):
        return ref[g] if kind == "p" else ref[...]

    def run_fwd(r_vals, c_vals):
        tile = _row_tile(s, r_w + o_w)

        def body(*refs):
            g = pl.program_id(1)
            rv = [refs[i][...] for i in range(n_r)]
            cv = [c_val(refs[n_r + i], c_kinds[i], g) for i in range(n_c)]
            res = fn(*rv, *cv)
            for i in range(n_o):
                refs[n_r + n_c + i][...] = res[i].astype(outs[i][1])

        res = pl.pallas_call(
            body, name=name + "_fwd", grid=(s // tile, groups),
            in_specs=[r_spec(i, tile) for i in range(n_r)] + [c_spec(i) for i in range(n_c)],
            out_specs=[o_spec(i, tile) for i in range(n_o)],
            out_shape=[jax.ShapeDtypeStruct((s, groups * o_w[i]), outs[i][1]) for i in range(n_o)],
            compiler_params=_params(("parallel", "arbitrary")))(*r_vals, *c_vals)
        return tuple(res)

    def run_bwd(r_vals, c_vals, cts):
        tile = _row_tile(s, r_w + o_w + o_w + [r_w[i] for i in diff_r])

        def body(*refs):
            r = pl.program_id(0)
            g = pl.program_id(1)
            in_refs = refs[:n_r + n_c + n_o]
            dr_refs = refs[n_r + n_c + n_o:n_r + n_c + n_o + len(diff_r)]
            dc_refs = refs[n_r + n_c + n_o + len(diff_r):]
            rv = [in_refs[i][...] for i in range(n_r)]
            cv = [c_val(in_refs[n_r + i], c_kinds[i], g) for i in range(n_c)]
            ct = tuple(in_refs[n_r + n_c + i][...].astype(f32) for i in range(n_o))

            @pl.when((r == 0) & (g == 0))
            def _():
                for d in dc_refs:
                    d[...] = jnp.zeros_like(d)

            def f(*dv):
                full = list(rv)
                for j, i in enumerate(diff_r):
                    full[i] = dv[j]
                return tuple(o.astype(f32) for o in fn(*full, *dv[len(diff_r):]))

            _, vjp = jax.vjp(f, *[rv[i] for i in diff_r], *cv)
            grads = vjp(ct)
            for j, i in enumerate(diff_r):
                if r_kinds[i][0] == "g" or groups == 1:
                    dr_refs[j][...] = grads[j]
                else:
                    @pl.when(g == 0)
                    def _(j=j):
                        dr_refs[j][...] = grads[j]

                    @pl.when(g > 0)
                    def _(j=j):
                        dr_refs[j][...] += grads[j]
            for i in range(n_c):
                gc = grads[len(diff_r) + i]
                if c_kinds[i] == "p":
                    dc_refs[i][g] += gc
                else:
                    dc_refs[i][...] += gc

        res = pl.pallas_call(
            body, name=name + "_bwd", grid=(s // tile, groups),
            in_specs=[r_spec(i, tile) for i in range(n_r)] + [c_spec(i) for i in range(n_c)]
            + [o_spec(i, tile) for i in range(n_o)],
            out_specs=[r_spec(i, tile) for i in diff_r] + [c_spec(i) for i in range(n_c)],
            out_shape=[jax.ShapeDtypeStruct(r_arrs[i].shape, f32) for i in diff_r]
            + [jax.ShapeDtypeStruct(c.shape, f32) for c in c_arrs],
            compiler_params=_params(("arbitrary", "arbitrary")))(*r_vals, *c_vals, *cts)
        return res[:len(diff_r)], res[len(diff_r):]

    @jax.custom_vjp
    def op(r_vals, c_vals):
        return run_fwd(r_vals, c_vals)

    def fwd(r_vals, c_vals):
        return run_fwd(r_vals, c_vals), (r_vals, c_vals)

    def bwd(res, cts):
        r_vals, c_vals = res
        d_r, d_c = run_bwd(r_vals, c_vals, cts)
        full = [jnp.zeros_like(v) for v in r_vals]
        for j, i in enumerate(diff_r):
            full[i] = d_r[j]
        return tuple(full), tuple(d_c)

    op.defvjp(fwd, bwd)
    return op(tuple(r_arrs), tuple(c_arrs))


def _rms(x, w):
    return x * lax.rsqrt(jnp.mean(x * x, axis=-1, keepdims=True) + EPS) * w


def rms_op(x, w, name, groups=1):
    return row_op(name, lambda x, w: (_rms(x, w),), [(x, "g")], [(w.reshape(1, -1), "c")],
                  [(x.shape[1] // groups, f32)], groups)[0]


def split_cols(x, widths):
    offs = [0]
    for w in widths:
        offs.append(offs[-1] + w)

    @jax.custom_vjp
    def op(x):
        return tuple(x[:, offs[i]:offs[i + 1]] for i in range(len(widths)))

    def fwd(x):
        return op(x), None

    def bwd(_, cts):
        return (jnp.concatenate(cts, axis=1),)

    op.defvjp(fwd, bwd)
    return op(x)


def rope_tables(positions):
    s = positions.shape[0]
    tile = min(s, 512)
    half = QK_ROPE // 2

    def body(p_ref, c_ref, s_ref):
        pos = p_ref[...].astype(f32)
        lane = lax.broadcasted_iota(jnp.int32, (1, QK_PAD), 1)
        idx = ((lane - QK_NOPE) % half).astype(f32)
        inv = jnp.exp(idx * (-math.log(ROPE_THETA) / half))
        ang = pos * inv
        rot = (lane >= QK_NOPE) & (lane < QK_NOPE + QK_ROPE)
        c_ref[...] = jnp.where(rot, jnp.cos(ang), jnp.where(lane < QK_NOPE, 1.0, 0.0))
        s_ref[...] = jnp.where(rot, jnp.sin(ang), 0.0)

    return pl.pallas_call(
        body, name="rope_tables", grid=(s // tile,), in_specs=[pl.BlockSpec((tile, 1), lambda i: (i, 0))],
        out_specs=[pl.BlockSpec((tile, QK_PAD), lambda i: (i, 0))] * 2,
        out_shape=[jax.ShapeDtypeStruct((s, QK_PAD), f32)] * 2,
        compiler_params=_params(("parallel",)))(positions)


ATT_ROW_GROUPS = 4


def _att_tile(s):
    return min(s, 1024)


def _att_fwd(q, k, v):
    s = q.shape[0]
    t = _att_tile(s)
    n = s // t

    qi_tab, kj_tab = _block_pairs(n, by_query=True)

    def body(qi_ref, kj_ref, q_ref, k_ref, v_ref, o_ref, lse_ref, m_sc, l_sc, acc_sc):
        pair = pl.program_id(1)
        qi = qi_ref[pair]
        kj = kj_ref[pair]

        @pl.when(kj == 0)
        def _():
            m_sc[...] = jnp.full_like(m_sc, NEG_BIG)
            l_sc[...] = jnp.zeros_like(l_sc)
            acc_sc[...] = jnp.zeros_like(acc_sc)

        def accumulate(diagonal):
            for rows in _row_groups(t):
                sc = _dot(q_ref[rows, :], k_ref[...], NT) * ATT_SCALE
                if diagonal:
                    sc = _mask_diagonal(sc, rows.start)
                m_old = m_sc[rows, :]
                m_new = jnp.maximum(m_old, jnp.max(sc, axis=1, keepdims=True))
                alpha = jnp.exp(m_old - m_new)
                p = jnp.exp(sc - m_new)
                l_sc[rows, :] = alpha * l_sc[rows, :] + jnp.sum(p, axis=1, keepdims=True)
                acc_sc[rows, :] = alpha * acc_sc[rows, :] + _dot(p.astype(bf16), v_ref[...], NN)
                m_sc[rows, :] = m_new

        @pl.when(kj < qi)
        def _():
            accumulate(False)

        @pl.when(kj == qi)
        def _():
            accumulate(True)
            o_ref[...] = acc_sc[...] / l_sc[...]
            lse_ref[...] = jnp.broadcast_to(m_sc[...] + jnp.log(l_sc[...]), (t, LANES))

    qmap = lambda h, p, qi_ref, kj_ref: (qi_ref[p], h)
    kmap = lambda h, p, qi_ref, kj_ref: (kj_ref[p], h)
    return pl.pallas_call(
        body, name="mla_att_fwd",
        grid_spec=pltpu.PrefetchScalarGridSpec(
            num_scalar_prefetch=2, grid=(MLA_HEADS, len(qi_tab)),
            in_specs=[pl.BlockSpec((t, QK_PAD), qmap), pl.BlockSpec((t, QK_PAD), kmap),
                      pl.BlockSpec((t, V_HEAD), kmap)],
            out_specs=[pl.BlockSpec((t, V_HEAD), qmap), pl.BlockSpec((t, LANES), qmap)],
            scratch_shapes=[pltpu.VMEM((t, 1), f32), pltpu.VMEM((t, 1), f32), pltpu.VMEM((t, V_HEAD), f32)]),
        out_shape=[jax.ShapeDtypeStruct((s, MLA_HEADS * V_HEAD), f32),
                   jax.ShapeDtypeStruct((s, MLA_HEADS * LANES), f32)],
        compiler_params=_params(("parallel", "arbitrary")))(qi_tab, kj_tab, q, k, v)


def _block_pairs(n, by_query):
    if by_query:
        pairs = [(i, j) for i in range(n) for j in range(i + 1)]
    else:
        pairs = [(i, j) for j in range(n) for i in range(j, n)]
    return (jnp.asarray([p[0] for p in pairs], jnp.int32), jnp.asarray([p[1] for p in pairs], jnp.int32))


def _mask_diagonal(sc, row0):
    rows = row0 + lax.broadcasted_iota(jnp.int32, sc.shape, 0)
    cols = lax.broadcasted_iota(jnp.int32, sc.shape, 1)
    return jnp.where(cols <= rows, sc, NEG_BIG)


def _row_groups(t):
    return [slice(rs, rs + t // ATT_ROW_GROUPS) for rs in range(0, t, t // ATT_ROW_GROUPS)]


def _att_probs(qs, k, lses, dos, os_, v, groups, diagonal):
    scs = [_dot(q, k, NT) * ATT_SCALE for q in qs]
    if diagonal:
        scs = [_mask_diagonal(sc, rows.start) for sc, rows in zip(scs, groups)]
    ps = _each(lambda sc, lse: jnp.exp(sc - lse[:, :1]), scs, lses)
    dps = [_dot(do, v, NT) for do in dos]
    deltas = _each(lambda do, o: jnp.sum(do.astype(f32) * o, axis=1, keepdims=True), dos, os_)
    dss = _each(lambda p, dp, delta: p * (dp - delta), ps, dps, deltas)
    return ps, dss


def _att_bwd_kv(q, k, v, o, lse, do):
    s = q.shape[0]
    t = _att_tile(s)
    n = s // t

    qi_tab, kj_tab = _block_pairs(n, by_query=False)

    def body(qi_ref, kj_ref, q_ref, k_ref, v_ref, o_ref, lse_ref, do_ref, dk_ref, dv_ref, dk_sc, dv_sc):
        pair = pl.program_id(1)
        qi = qi_ref[pair]
        kj = kj_ref[pair]

        def accumulate(diagonal):
            dv = dk = None
            for rows in _row_groups(t):
                (p,), (ds,) = _att_probs([q_ref[rows, :]], k_ref[...], [lse_ref[rows, :]], [do_ref[rows, :]],
                                         [o_ref[rows, :]], v_ref[...], [rows], diagonal)
                dv_g = _dot(p.astype(bf16), do_ref[rows, :], TN)
                dk_g = _dot(ds.astype(bf16), q_ref[rows, :], TN)
                dv = dv_g if dv is None else dv + dv_g
                dk = dk_g if dk is None else dk + dk_g
            return dv, dk

        @pl.when(qi == kj)
        def _():
            dv, dk = accumulate(True)
            dv_sc[...] = dv
            dk_sc[...] = dk

        @pl.when(qi > kj)
        def _():
            dv, dk = accumulate(False)
            dv_sc[...] += dv
            dk_sc[...] += dk

        @pl.when(qi == n - 1)
        def _():
            dk_ref[...] = dk_sc[...] * ATT_SCALE
            dv_ref[...] = dv_sc[...]

    qmap = lambda h, p, qi_ref, kj_ref: (qi_ref[p], h)
    kmap = lambda h, p, qi_ref, kj_ref: (kj_ref[p], h)
    return pl.pallas_call(
        body, name="mla_att_bwd_kv",
        grid_spec=pltpu.PrefetchScalarGridSpec(
            num_scalar_prefetch=2, grid=(MLA_HEADS, len(qi_tab)),
            in_specs=[pl.BlockSpec((t, QK_PAD), qmap), pl.BlockSpec((t, QK_PAD), kmap), pl.BlockSpec((t, V_HEAD), kmap),
                      pl.BlockSpec((t, V_HEAD), qmap), pl.BlockSpec((t, LANES), qmap), pl.BlockSpec((t, V_HEAD), qmap)],
            out_specs=[pl.BlockSpec((t, QK_PAD), kmap), pl.BlockSpec((t, V_HEAD), kmap)],
            scratch_shapes=[pltpu.VMEM((t, QK_PAD), f32), pltpu.VMEM((t, V_HEAD), f32)]),
        out_shape=[jax.ShapeDtypeStruct((s, MLA_HEADS * QK_PAD), f32), jax.ShapeDtypeStruct((s, MLA_HEADS * V_HEAD), f32)],
        compiler_params=_params(("parallel", "arbitrary")))(qi_tab, kj_tab, q, k, v, o, lse, do)


def _att_bwd_q(q, k, v, o, lse, do):
    s = q.shape[0]
    t = _att_tile(s)
    n = s // t

    qi_tab, kj_tab = _block_pairs(n, by_query=True)

    def body(qi_ref, kj_ref, q_ref, k_ref, v_ref, o_ref, lse_ref, do_ref, dq_ref, dq_sc):
        pair = pl.program_id(1)
        qi = qi_ref[pair]
        kj = kj_ref[pair]

        @pl.when(kj == 0)
        def _():
            dq_sc[...] = jnp.zeros_like(dq_sc)

        def accumulate(diagonal):
            for rows in _row_groups(t):
                _, (ds,) = _att_probs([q_ref[rows, :]], k_ref[...], [lse_ref[rows, :]], [do_ref[rows, :]],
                                      [o_ref[rows, :]], v_ref[...], [rows], diagonal)
                dq_sc[rows, :] += _dot(ds.astype(bf16), k_ref[...], NN)

        @pl.when(kj < qi)
        def _():
            accumulate(False)

        @pl.when(kj == qi)
        def _():
            accumulate(True)
            dq_ref[...] = dq_sc[...] * ATT_SCALE

    qmap = lambda h, p, qi_ref, kj_ref: (qi_ref[p], h)
    kmap = lambda h, p, qi_ref, kj_ref: (kj_ref[p], h)
    return pl.pallas_call(
        body, name="mla_att_bwd_q",
        grid_spec=pltpu.PrefetchScalarGridSpec(
            num_scalar_prefetch=2, grid=(MLA_HEADS, len(qi_tab)),
            in_specs=[pl.BlockSpec((t, QK_PAD), qmap), pl.BlockSpec((t, QK_PAD), kmap), pl.BlockSpec((t, V_HEAD), kmap),
                      pl.BlockSpec((t, V_HEAD), qmap), pl.BlockSpec((t, LANES), qmap), pl.BlockSpec((t, V_HEAD), qmap)],
            out_specs=pl.BlockSpec((t, QK_PAD), qmap),
            scratch_shapes=[pltpu.VMEM((t, QK_PAD), f32)]),
        out_shape=jax.ShapeDtypeStruct((s, MLA_HEADS * QK_PAD), f32),
        compiler_params=_params(("parallel", "arbitrary")))(qi_tab, kj_tab, q, k, v, o, lse, do)


@jax.custom_vjp
def attention(q, k, v):
    return _att_fwd(q.astype(bf16), k.astype(bf16), v.astype(bf16))[0]


def _attention_fwd(q, k, v):
    qb, kb, vb = q.astype(bf16), k.astype(bf16), v.astype(bf16)
    o, lse = _att_fwd(qb, kb, vb)
    return o, (qb, kb, vb, o, lse)


def _attention_bwd(res, do):
    qb, kb, vb, o, lse = res
    dob = do.astype(bf16)
    dk, dv = _att_bwd_kv(qb, kb, vb, o, lse, dob)
    dq = _att_bwd_q(qb, kb, vb, o, lse, dob)
    return dq, dk, dv


attention.defvjp(_attention_fwd, _attention_bwd)


def _seq_tile(s):
    return min(s, 256)


def _chan_tile(c):
    return _largest_tile(c, 512, LANES)


def _shift_down(ext, sh, t):
    if sh == 0:
        return ext[SUBLANES:]
    return pltpu.roll(ext, sh, axis=0)[SUBLANES:]


def _conv_fwd_call(x, w, b, name):
    s, c = x.shape
    t, cb = _seq_tile(s), _chan_tile(c)
    hb = t // SUBLANES

    def body(x_ref, xp_ref, w_ref, b_ref, y_ref):
        ti = pl.program_id(1)
        prev = jnp.where(ti == 0, 0.0, xp_ref[...])
        ext = jnp.concatenate([prev, x_ref[...]], axis=0)
        acc = jnp.broadcast_to(b_ref[...], (t, cb))
        for kk in range(CONV_W):
            acc = acc + w_ref[kk:kk + 1, :] * _shift_down(ext, CONV_W - 1 - kk, t)
        y_ref[...] = acc

    return pl.pallas_call(
        body, name=name + "_fwd", grid=(c // cb, s // t),
        in_specs=[pl.BlockSpec((t, cb), lambda ci, ti: (ti, ci)),
                  pl.BlockSpec((SUBLANES, cb), lambda ci, ti: (jnp.maximum(ti * hb - 1, 0), ci)),
                  pl.BlockSpec((CONV_W, cb), lambda ci, ti: (0, ci)),
                  pl.BlockSpec((1, cb), lambda ci, ti: (0, ci))],
        out_specs=pl.BlockSpec((t, cb), lambda ci, ti: (ti, ci)),
        out_shape=jax.ShapeDtypeStruct((s, c), f32),
        compiler_params=_params(("parallel", "arbitrary")))(x, x, w, b)


def _conv_bwd_call(x, w, dy, name):
    s, c = x.shape
    t, cb = _seq_tile(s), _chan_tile(c)
    hb = t // SUBLANES
    nt = s // t

    def body(x_ref, xp_ref, w_ref, dy_ref, dyn_ref, dx_ref, dw_ref, db_ref):
        ti = pl.program_id(1)

        @pl.when(ti == 0)
        def _():
            dw_ref[...] = jnp.zeros_like(dw_ref)
            db_ref[...] = jnp.zeros_like(db_ref)

        dy = dy_ref[...]
        nxt = jnp.where(ti == nt - 1, 0.0, dyn_ref[...])
        dext = jnp.concatenate([dy, nxt], axis=0)
        prev = jnp.where(ti == 0, 0.0, xp_ref[...])
        xext = jnp.concatenate([prev, x_ref[...]], axis=0)
        dx = jnp.zeros((t, cb), f32)
        for kk in range(CONV_W):
            sh = CONV_W - 1 - kk
            up = dext[:t] if sh == 0 else pltpu.roll(dext, t + SUBLANES - sh, axis=0)[:t]
            dx = dx + w_ref[kk:kk + 1, :] * up
            dw_ref[kk:kk + 1, :] += jnp.sum(dy * _shift_down(xext, sh, t), axis=0, keepdims=True)
        dx_ref[...] = dx
        db_ref[...] += jnp.sum(dy, axis=0, keepdims=True)

    return pl.pallas_call(
        body, name=name + "_bwd", grid=(c // cb, nt),
        in_specs=[pl.BlockSpec((t, cb), lambda ci, ti: (ti, ci)),
                  pl.BlockSpec((SUBLANES, cb), lambda ci, ti: (jnp.maximum(ti * hb - 1, 0), ci)),
                  pl.BlockSpec((CONV_W, cb), lambda ci, ti: (0, ci)),
                  pl.BlockSpec((t, cb), lambda ci, ti: (ti, ci)),
                  pl.BlockSpec((SUBLANES, cb), lambda ci, ti: (jnp.minimum((ti + 1) * hb, nt * hb - 1), ci))],
        out_specs=[pl.BlockSpec((t, cb), lambda ci, ti: (ti, ci)),
                   pl.BlockSpec((CONV_W, cb), lambda ci, ti: (0, ci)),
                   pl.BlockSpec((1, cb), lambda ci, ti: (0, ci))],
        out_shape=[jax.ShapeDtypeStruct((s, c), f32), jax.ShapeDtypeStruct((CONV_W, c), f32),
                   jax.ShapeDtypeStruct((1, c), f32)],
        compiler_params=_params(("parallel", "arbitrary")))(x, x, w, dy, dy)


def causal_conv(x, w, b, name):
    @jax.custom_vjp
    def op(x, w, b):
        return _conv_fwd_call(x, w, b, name)

    def fwd(x, w, b):
        return op(x, w, b), (x, w)

    def bwd(res, dy):
        x, w = res
        dx, dw, db = _conv_bwd_call(x, w, dy, name)
        return dx, dw, db

    op.defvjp(fwd, bwd)
    return op(x, w, b)


def _scan_fwd_call(a, b):
    s, c = a.shape
    t, cb = _seq_tile(s), _chan_tile(c)

    def body(a_ref, b_ref, h_ref, carry):
        ti = pl.program_id(1)

        @pl.when(ti == 0)
        def _():
            carry[...] = jnp.zeros_like(carry)

        av, bv = a_ref[...], b_ref[...]
        rows = lax.broadcasted_iota(jnp.int32, (t, cb), 0)
        d = 1
        while d < t:
            a_sh = jnp.where(rows >= d, pltpu.roll(av, d, axis=0), 1.0)
            b_sh = jnp.where(rows >= d, pltpu.roll(bv, d, axis=0), 0.0)
            bv = av * b_sh + bv
            av = av * a_sh
            d *= 2
        h = av * carry[0:1, :] + bv
        h_ref[...] = h
        carry[0:1, :] = h[t - 1:t, :]

    return pl.pallas_call(
        body, name="lru_scan_fwd", grid=(c // cb, s // t),
        in_specs=[pl.BlockSpec((t, cb), lambda ci, ti: (ti, ci))] * 2,
        out_specs=pl.BlockSpec((t, cb), lambda ci, ti: (ti, ci)),
        out_shape=jax.ShapeDtypeStruct((s, c), f32),
        scratch_shapes=[pltpu.VMEM((SUBLANES, cb), f32)],
        compiler_params=_params(("parallel", "arbitrary")))(a, b)


def _scan_bwd_call(a, h, dh):
    s, c = a.shape
    t, cb = _seq_tile(s), _chan_tile(c)
    hb = t // SUBLANES
    nt = s // t

    def body(a_ref, an_ref, h_ref, hp_ref, dh_ref, da_ref, db_ref, carry):
        step = pl.program_id(1)
        ti = nt - 1 - step

        @pl.when(step == 0)
        def _():
            carry[...] = jnp.zeros_like(carry)

        rows = lax.broadcasted_iota(jnp.int32, (t, cb), 0)
        av = a_ref[...]
        an = jnp.where(rows == t - 1, an_ref[0:1, :], pltpu.roll(av, t - 1, axis=0))
        gv = dh_ref[...]
        d = 1
        while d < t:
            a_sh = jnp.where(rows < t - d, pltpu.roll(an, t - d, axis=0), 1.0)
            g_sh = jnp.where(rows < t - d, pltpu.roll(gv, t - d, axis=0), 0.0)
            gv = an * g_sh + gv
            an = an * a_sh
            d *= 2
        g = an * carry[0:1, :] + gv
        carry[0:1, :] = g[0:1, :]
        hv = h_ref[...]
        first = jnp.where(ti == 0, 0.0, hp_ref[SUBLANES - 1:SUBLANES, :])
        h_prev = jnp.where(rows == 0, first, pltpu.roll(hv, 1, axis=0))
        da_ref[...] = g * h_prev
        db_ref[...] = g

    cur = lambda ci, st: (nt - 1 - st, ci)
    return pl.pallas_call(
        body, name="lru_scan_bwd", grid=(c // cb, nt),
        in_specs=[pl.BlockSpec((t, cb), cur),
                  pl.BlockSpec((SUBLANES, cb), lambda ci, st: (jnp.minimum((nt - st) * hb, nt * hb - 1), ci)),
                  pl.BlockSpec((t, cb), cur),
                  pl.BlockSpec((SUBLANES, cb), lambda ci, st: (jnp.maximum((nt - 1 - st) * hb - 1, 0), ci)),
                  pl.BlockSpec((t, cb), cur)],
        out_specs=[pl.BlockSpec((t, cb), cur)] * 2,
        out_shape=[jax.ShapeDtypeStruct((s, c), f32)] * 2,
        scratch_shapes=[pltpu.VMEM((SUBLANES, cb), f32)],
        compiler_params=_params(("parallel", "arbitrary")))(a, a, h, h, dh)


@jax.custom_vjp
def lru_scan(a, b):
    return _scan_fwd_call(a, b)


def _lru_scan_fwd(a, b):
    h = _scan_fwd_call(a, b)
    return h, (a, h)


def _lru_scan_bwd(res, dh):
    a, h = res
    da, db = _scan_bwd_call(a, h, dh)
    return da, db


lru_scan.defvjp(_lru_scan_fwd, _lru_scan_bwd)


def _expm1(x):
    small = x * (1.0 + x / 2.0 * (1.0 + x / 3.0 * (1.0 + x / 4.0 * (1.0 + x / 5.0 * (1.0 + x / 6.0 * (1.0 + x / 7.0))))))
    return jnp.where(jnp.abs(x) < 0.25, small, jnp.exp(x) - 1.0)


def _lru_gates(xc, w_a, b_a, w_x, b_x, lam):
    xb = xc.astype(bf16)
    r = jax.nn.sigmoid(_dot(xb, w_a.astype(bf16), NN) + b_a)
    gi = jax.nn.sigmoid(_dot(xb, w_x.astype(bf16), NN) + b_x)
    log_a = -LRU_C * r * jax.nn.softplus(-lam)
    a = jnp.exp(log_a)
    mult = jnp.sqrt(-_expm1(2.0 * log_a))
    return a, mult * (gi * xc)


def _each(fn, *lists):
    return [fn(*items) for items in zip(*lists)]


@jax.custom_vjp
def _inv_unit_lower(ls):
    c = ls[0].shape[0]
    eye = (lax.broadcasted_iota(jnp.int32, (c, c), 0) == lax.broadcasted_iota(jnp.int32, (c, c), 1)).astype(f32)
    ps = [eye - l for l in ls]
    ms = list(ls)
    span = 1
    while 2 * span < c:
        ms = [_dot(m, m, NN, MID) for m in ms]
        ps = _each(lambda p, m: p + _dot(p, m, NN, MID), ps, ms)
        span *= 2
    return tuple(ps)


def _inv_unit_lower_fwd(ls):
    ts = _inv_unit_lower(ls)
    return ts, ts


def _inv_unit_lower_bwd(ts, dts):
    inner = _each(lambda t, dt: _dot(t, dt, TN, MID), ts, dts)
    return (tuple(_each(lambda x, t: -_dot(x, t, NT, MID), inner, ts)),)


_inv_unit_lower.defvjp(_inv_unit_lower_fwd, _inv_unit_lower_bwd)


def _gdn_chunk(states, qs, ks, vs, gfull, bfull):
    c = GDN_CHUNK
    heads = list(range(len(states)))
    lane = lax.broadcasted_iota(jnp.int32, (c, LANES), 1)
    row = lax.broadcasted_iota(jnp.int32, (c, LANES), 0)
    r = lax.broadcasted_iota(jnp.int32, (c, c), 0)
    cc = lax.broadcasted_iota(jnp.int32, (c, c), 1)
    tri = (r >= cc).astype(f32)
    first_lane = (lane == 0).astype(f32)
    gs = [jnp.sum(jnp.where(lane == GDN_HEADS + h, gfull, 0.0), axis=1, keepdims=True) for h in heads]
    betas = [jnp.sum(jnp.where(lane == h, bfull, 0.0), axis=1, keepdims=True) for h in heads]
    gcs = [_dot(tri, jnp.broadcast_to(g, (c, LANES)), NN, HI) for g in gs]
    gc_cols = [_dot(first_lane, gc, NT, HI) for gc in gcs]
    gc_rows = [jnp.sum(jnp.where(lane == 0, gc, 0.0), axis=1, keepdims=True) for gc in gcs]
    decays = _each(lambda gr, gcl: jnp.exp(jnp.where(r >= cc, gr - gcl, NEG_BIG)), gc_rows, gc_cols)
    qs = [q * GDN_DK ** -0.5 for q in qs]
    k_betas = _each(lambda k, b: k * b, ks, betas)
    v_betas = _each(lambda v, b: v * b, vs, betas)
    egcs = [jnp.exp(gc) for gc in gcs]
    kkts = _each(lambda kb, k, d: _bdot(kb, k, NT) * d, k_betas, ks, decays)
    ts = _inv_unit_lower(tuple(jnp.where(r > cc, kkt, 0.0) for kkt in kkts))
    us = _each(lambda t, vb: _dot(t, vb, NN, MID), ts, v_betas)
    ws = _each(lambda t, kb, e: _dot(t, kb * e, NN, MID), ts, k_betas, egcs)
    qks = _each(lambda q, k, d: jnp.where(r >= cc, _bdot(q, k, NT) * d, 0.0), qs, ks, decays)
    gls = [jnp.sum(jnp.where(row == c - 1, gc, 0.0), axis=0, keepdims=True) for gc in gcs]
    k_tails = _each(lambda k, gl, gc: k * jnp.exp(gl - gc), ks, gls, gcs)
    v_news = _each(lambda u, w, st: u - _bdot(w, st, NN), us, ws, states)
    os_ = _each(lambda q, e, st, qk, vn: _bdot(q * e, st, NN) + _bdot(qk, vn, NN), qs, egcs, states, qks, v_news)
    new_states = _each(lambda st, gl, kt, vn: st * jnp.exp(gl) + _bdot(kt, vn, TN), states, gls, k_tails, v_news)
    return tuple(os_), tuple(new_states)


def _head_cols(h):
    return slice(h * LANES, (h + 1) * LANES)


def _gdn_fwd_call(q, k, v, gfull, bfull):
    s = q.shape[0]
    c = GDN_CHUNK
    n = s // c

    def body(q_ref, k_ref, v_ref, g_ref, b_ref, o_ref, st_ref, state):
        @pl.when(pl.program_id(0) == 0)
        def _():
            state[...] = jnp.zeros_like(state)

        heads = range(GDN_HEADS)
        states = [state[h] for h in heads]
        outs, new_states = _gdn_chunk(states, *[[ref[:, _head_cols(h)] for h in heads] for ref in (q_ref, k_ref, v_ref)],
                                      g_ref[...], b_ref[...])
        for h in heads:
            st_ref[0, h] = states[h]
            o_ref[:, _head_cols(h)] = outs[h]
            state[h] = new_states[h]

    hd = pl.BlockSpec((c, GDN_HEADS * LANES), lambda ni: (ni, 0))
    sh = pl.BlockSpec((c, LANES), lambda ni: (ni, 0))
    return pl.pallas_call(
        body, name="gdn_fwd", grid=(n,), in_specs=[hd, hd, hd, sh, sh],
        out_specs=[hd, pl.BlockSpec((1, GDN_HEADS, GDN_DK, GDN_DV), lambda ni: (ni, 0, 0, 0))],
        out_shape=[jax.ShapeDtypeStruct((s, GDN_HEADS * GDN_DV), f32),
                   jax.ShapeDtypeStruct((n, GDN_HEADS, GDN_DK, GDN_DV), f32)],
        scratch_shapes=[pltpu.VMEM((GDN_HEADS, GDN_DK, GDN_DV), f32)],
        compiler_params=_params(("arbitrary",)))(q, k, v, gfull, bfull)


def _gdn_bwd_call(q, k, v, gfull, bfull, states, do):
    s = q.shape[0]
    c = GDN_CHUNK
    n = s // c

    def body(q_ref, k_ref, v_ref, g_ref, b_ref, st_ref, do_ref, dq_ref, dk_ref, dv_ref, dg_ref, db_ref, dstate):
        @pl.when(pl.program_id(0) == 0)
        def _():
            dstate[...] = jnp.zeros_like(dstate)

        heads = range(GDN_HEADS)
        per_head = [tuple(ref[:, _head_cols(h)] for h in heads) for ref in (q_ref, k_ref, v_ref)]
        _, vjp = jax.vjp(_gdn_chunk, tuple(st_ref[0, h] for h in heads), *per_head, g_ref[...], b_ref[...])
        ds0, dq, dk, dv, dg, db = vjp((tuple(do_ref[:, _head_cols(h)] for h in heads), tuple(dstate[h] for h in heads)))
        for h in heads:
            dstate[h] = ds0[h]
            dq_ref[:, _head_cols(h)] = dq[h]
            dk_ref[:, _head_cols(h)] = dk[h]
            dv_ref[:, _head_cols(h)] = dv[h]
        dg_ref[...] = dg
        db_ref[...] = db

    hd = pl.BlockSpec((c, GDN_HEADS * LANES), lambda st: (n - 1 - st, 0))
    sh = pl.BlockSpec((c, LANES), lambda st: (n - 1 - st, 0))
    big = jax.ShapeDtypeStruct((s, GDN_HEADS * GDN_DV), f32)
    small = jax.ShapeDtypeStruct((s, LANES), f32)
    return pl.pallas_call(
        body, name="gdn_bwd", grid=(n,),
        in_specs=[hd, hd, hd, sh, sh,
                  pl.BlockSpec((1, GDN_HEADS, GDN_DK, GDN_DV), lambda st: (n - 1 - st, 0, 0, 0)), hd],
        out_specs=[hd, hd, hd, sh, sh], out_shape=[big, big, big, small, small],
        scratch_shapes=[pltpu.VMEM((GDN_HEADS, GDN_DK, GDN_DV), f32)],
        compiler_params=_params(("arbitrary",)))(q, k, v, gfull, bfull, states, do)


@jax.custom_vjp
def gdn_core(q, k, v, gfull, bfull):
    return _gdn_fwd_call(q, k, v, gfull, bfull)[0]


def _gdn_core_fwd(q, k, v, gfull, bfull):
    o, states = _gdn_fwd_call(q, k, v, gfull, bfull)
    return o, (q, k, v, gfull, bfull, states)


def _gdn_core_bwd(res, do):
    return tuple(_gdn_bwd_call(*res, do))


gdn_core.defvjp(_gdn_core_fwd, _gdn_core_bwd)


def _l2norm(t):
    return t * lax.rsqrt(jnp.sum(t * t, axis=-1, keepdims=True) + EPS)


def _gdn_pre(qc, kc, vc):
    return _l2norm(jax.nn.silu(qc)), _l2norm(jax.nn.silu(kc)), jax.nn.silu(vc)


def _gdn_gates(gba, a_log, dt_bias):
    beta = jax.nn.sigmoid(gba)
    g = -jnp.exp(a_log) * jax.nn.softplus(gba + dt_bias)
    return beta, g


def _gdn_post(o, z, w):
    return (_rms(o, w) * jax.nn.silu(z),)


def _merge(u0, u1, u2, g0, g1, g2, b0, b1, b2):
    return (jax.nn.sigmoid(g0 + b0) * u0 + jax.nn.sigmoid(g1 + b1) * u1 + jax.nn.sigmoid(g2 + b2) * u2,)


def _rope_q(qa, qb, ctab, stab):
    return (qa * ctab + qb * stab,)


def _rope_k(kp, kra, krb, ctab, stab):
    return (kp + kra * ctab + krb * stab,)


def _swap_rot(w):
    half = QK_ROPE // 2
    return jnp.concatenate([-w[..., half:], w[..., :half]], axis=-1)


def _ext_w_in(w):
    d = w.shape[0]
    z = lambda n: jnp.zeros((d, n), w.dtype)
    kr = w[:, 1024:1088]
    return jnp.concatenate([
        w[:, :1024], z(QK_NOPE), kr, z(64), z(QK_NOPE), _swap_rot(kr), z(64),
        w[:, 1088:7232], w[:, 7232:7248], z(112), w[:, 7248:], z(128)], axis=1)


def _ext_w_q(w):
    d = w.shape[0]
    w3 = w.reshape(d, MLA_HEADS, QK_NOPE + QK_ROPE)
    nope, pe = w3[..., :QK_NOPE], w3[..., QK_NOPE:]
    z64 = jnp.zeros((d, MLA_HEADS, 64), w.dtype)
    z128 = jnp.zeros((d, MLA_HEADS, QK_NOPE), w.dtype)
    a = jnp.concatenate([nope, pe, z64], axis=-1).reshape(d, MLA_HEADS * QK_PAD)
    b = jnp.concatenate([z128, _swap_rot(pe), z64], axis=-1).reshape(d, MLA_HEADS * QK_PAD)
    return jnp.concatenate([a, b], axis=1)


def _ext_w_kv(w):
    d = w.shape[0]
    w3 = w.reshape(d, MLA_HEADS, QK_NOPE + V_HEAD)
    kn, v = w3[..., :QK_NOPE], w3[..., QK_NOPE:]
    kp = jnp.concatenate([kn, jnp.zeros_like(kn)], axis=-1).reshape(d, MLA_HEADS * QK_PAD)
    return jnp.concatenate([kp, v.reshape(d, MLA_HEADS * V_HEAD)], axis=1)


def _lane_pad(vec8):
    return jnp.concatenate([jnp.zeros((8,), f32), vec8, jnp.zeros((LANES - 16,), f32)]).reshape(1, LANES)


def layer(x, p, ctab, stab, li):
    tag = f"l{li}_"
    h = rms_op(x, p["norm_mix"], tag + "norm_mix")
    proj = linear(h, _ext_w_in(p["w_in"]), tag + "w_in")
    (c_q, c_kv, kra, krb, lru_x, lru_y, g_q, g_k, g_v, g_z, g_ba, gl0, gl1, gl2, _) = split_cols(
        proj, [w for _, w in IN_PIECES])

    cqn = rms_op(c_q, p["mla_q_norm"], tag + "q_norm")
    ckvn = rms_op(c_kv, p["mla_kv_norm"], tag + "kv_norm")
    qall = linear(cqn, _ext_w_q(p["mla_w_uq"]), tag + "w_uq")
    kvall = linear(ckvn, _ext_w_kv(p["mla_w_ukv"]), tag + "w_ukv")
    qa, qb = split_cols(qall, [MLA_HEADS * QK_PAD] * 2)
    kp, v = split_cols(kvall, [MLA_HEADS * QK_PAD, MLA_HEADS * V_HEAD])
    q = row_op(tag + "rope_q", _rope_q, [(qa, "g"), (qb, "g"), (ctab, "sn"), (stab, "sn")], [], [(QK_PAD, f32)],
               MLA_HEADS)[0]
    k = row_op(tag + "rope_k", _rope_k, [(kp, "g"), (kra, "s"), (krb, "s"), (ctab, "sn"), (stab, "sn")], [],
               [(QK_PAD, f32)], MLA_HEADS)[0]
    y_mla = attention(q, k, v)

    xc = causal_conv(lru_x, p["lru_conv_w"], p["lru_conv_b"].reshape(1, -1), tag + "lru_conv")
    gshape = (LRU_BLOCKS, 1, LRU_BLOCK_W)
    a, bx = row_op(tag + "lru_gates", _lru_gates, [(xc, "g")],
                   [(p["lru_w_a"], "p"), (p["lru_b_a"].reshape(gshape), "p"), (p["lru_w_x"], "p"),
                    (p["lru_b_x"].reshape(gshape), "p"), (p["lru_lambda"].reshape(gshape), "p")],
                   [(LRU_BLOCK_W, f32)] * 2, LRU_BLOCKS)
    hs = lru_scan(a, bx)
    y_lru = row_op(tag + "lru_out", lambda hh, yy: (hh * jax.nn.gelu(yy),), [(hs, "g"), (lru_y, "g")], [],
                   [(LRU_WIDTH, f32)])[0]

    cw = p["gdn_conv_w"]
    nob = jnp.zeros((1, GDN_HEADS * GDN_DK), f32)
    qc = causal_conv(g_q, cw[:, :1024], nob, tag + "gdn_conv_q")
    kc = causal_conv(g_k, cw[:, 1024:2048], nob, tag + "gdn_conv_k")
    vc = causal_conv(g_v, cw[:, 2048:], nob, tag + "gdn_conv_v")
    qn, kn, vs = row_op(tag + "gdn_pre", _gdn_pre, [(qc, "g"), (kc, "g"), (vc, "g")], [], [(GDN_DK, f32)] * 3,
                        GDN_HEADS)
    bfull, gfull = row_op(tag + "gdn_gates", _gdn_gates, [(g_ba, "g")],
                          [(_lane_pad(p["gdn_a_log"]), "c"), (_lane_pad(p["gdn_dt_bias"]), "c")], [(LANES, f32)] * 2)
    o = gdn_core(qn, kn, vs, gfull, bfull)
    y_gdn = row_op(tag + "gdn_post", _gdn_post, [(o, "g"), (g_z, "g")], [(p["gdn_norm"].reshape(1, -1), "c")],
                   [(GDN_DV, f32)], GDN_HEADS)[0]

    wb = p["w_branch"]
    u0 = linear(y_mla, wb[0], tag + "w_branch0")
    u1 = linear(y_lru, wb[1], tag + "w_branch1")
    u2 = linear(y_gdn, wb[2], tag + "w_branch2")
    mixed = row_op(tag + "merge", _merge, [(u0, "g"), (u1, "g"), (u2, "g"), (gl0, "g"), (gl1, "g"), (gl2, "g")],
                   [(p["b_gate"][nb:nb + 1], "c") for nb in range(N_BRANCH)], [(D_MODEL, f32)])[0]
    x = x + linear(mixed, p["w_out"], tag + "w_out")

    h2 = rms_op(x, p["norm_ffn"], tag + "norm_ffn")
    gt = linear(h2, p["ffn_w_gate"], tag + "ffn_gate")
    up = linear(h2, p["ffn_w_up"], tag + "ffn_up")
    act = row_op(tag + "swiglu", lambda a_, b_: (jax.nn.silu(a_) * b_,), [(gt, "g"), (up, "g")], [],
                 [(FFN_HIDDEN, f32)])[0]
    return x + linear(act, p["ffn_w_down"], tag + "ffn_down")


def trunk(layers, x, ctab, stab):
    for li, p in enumerate(layers):
        x = layer(x, p, ctab, stab, li)
    return x


def loss_head(x, target, w):
    s, d = x.shape
    tile = _row_tile(s, [d] * 3)

    def fn(xv, wv, tv):
        err = jnp.square(_rms(xv, wv) - tv)
        return 0.5 * jnp.sum(jnp.mean(err, axis=-1, keepdims=True), axis=0, keepdims=True)

    def body(x_ref, t_ref, w_ref, loss_ref, dx_ref, dw_ref):
        @pl.when(pl.program_id(0) == 0)
        def _():
            loss_ref[...] = jnp.zeros_like(loss_ref)
            dw_ref[...] = jnp.zeros_like(dw_ref)

        tv = t_ref[...]
        val, vjp = jax.vjp(lambda xv, wv: fn(xv, wv, tv), x_ref[...], w_ref[...])
        dx, dw = vjp(jnp.ones((1, 1), f32))
        loss_ref[...] += jnp.broadcast_to(val, loss_ref.shape)
        dx_ref[...] = dx
        dw_ref[...] += dw

    loss, dx, dw = pl.pallas_call(
        body, name="loss_head", grid=(s // tile,),
        in_specs=[pl.BlockSpec((tile, d), lambda i: (i, 0)), pl.BlockSpec((tile, d), lambda i: (i, 0)),
                  pl.BlockSpec((1, d), lambda i: (0, 0))],
        out_specs=[pl.BlockSpec((SUBLANES, LANES), lambda i: (0, 0)), pl.BlockSpec((tile, d), lambda i: (i, 0)),
                   pl.BlockSpec((1, d), lambda i: (0, 0))],
        out_shape=[jax.ShapeDtypeStruct((SUBLANES, LANES), f32), jax.ShapeDtypeStruct((s, d), f32),
                   jax.ShapeDtypeStruct((1, d), f32)],
        compiler_params=_params(("arbitrary",)))(x, target, w.reshape(1, d))
    return loss[0, 0], dx, dw.reshape(d)


def local_step(layers, norm_final, x, positions, target):
    ctab, stab = rope_tables(positions.reshape(-1, 1))
    y, pull = jax.vjp(lambda ls, xx: trunk(ls, xx, ctab, stab), layers, x)
    loss, dy, d_final = loss_head(y, target, norm_final)
    d_layers, dx = pull(dy)
    return loss, dx, d_layers, d_final


def _flat2d(a):
    return a.reshape(-1, a.shape[-1])


def _ew_tile(rows, cols, n_arrays):
    budget = 16 * 1024 * 1024
    cap = max(SUBLANES, budget // (2 * 4 * cols * n_arrays))
    if rows <= cap:
        return rows
    return _largest_tile(rows, cap, SUBLANES)


def elementwise(name, fn, arrays, out_dtypes):
    shape = arrays[0].shape
    flat = [_flat2d(a) for a in arrays]
    rows, cols = flat[0].shape
    tile = _ew_tile(rows, cols, len(arrays) + len(out_dtypes))
    n_in = len(arrays)

    def body(*refs):
        res = fn(*[r[...] for r in refs[:n_in]])
        for o_ref, o in zip(refs[n_in:], res):
            o_ref[...] = o.astype(o_ref.dtype)

    spec = pl.BlockSpec((tile, cols), lambda i: (i, 0))
    res = pl.pallas_call(
        body, name=name, grid=(rows // tile,), in_specs=[spec] * n_in, out_specs=[spec] * len(out_dtypes),
        out_shape=[jax.ShapeDtypeStruct((rows, cols), dt) for dt in out_dtypes],
        compiler_params=_params(("parallel",)))(*flat)
    return [r.reshape(shape) for r in res]


def _adamw(w, g, m, v):
    m = ADAM_B1 * m + (1.0 - ADAM_B1) * g
    v = ADAM_B2 * v + (1.0 - ADAM_B2) * jnp.square(g)
    m_hat = m / (1.0 - ADAM_B1 ** ADAM_STEP)
    v_hat = v / (1.0 - ADAM_B2 ** ADAM_STEP)
    delta = -ADAM_LR * (m_hat / (jnp.sqrt(v_hat) + ADAM_EPS) + ADAM_WD * w)
    return delta, m, v


def _place():
    x, y, c = lax.axis_index("x"), lax.axis_index("y"), lax.axis_index("c")
    return x, y, c, 2 * x + y


CHIP_FLIPS = ((1, 0), (0, 1), (1, 1))


def _hbm_specs(n):
    return [pl.BlockSpec(memory_space=pl.ANY)] * n


def gather_weights(shards, name):
    n = len(shards)

    def body(*refs):
        src = refs[:n]
        out = refs[n:2 * n]
        send1, recv1, send2, recv2, loc = refs[2 * n:]
        x, y, c, j = _place()
        mine = pl.ds(c * HALF_LAYERS, HALF_LAYERS)
        other = pl.ds((1 - c) * HALF_LAYERS, HALF_LAYERS)
        locals_ = []
        for a in range(n):
            for hh, sl in ((c, mine), (1 - c, other)):
                cp = pltpu.make_async_copy(src[a].at[sl], out[a].at[hh, j], loc.at[2 * a + (0 if sl is mine else 1)])
                cp.start()
                locals_.append(cp)
        firsts = []
        for a in range(n):
            for kk, (fx, fy) in enumerate(CHIP_FLIPS):
                cp = pltpu.make_async_remote_copy(
                    src_ref=src[a].at[mine], dst_ref=out[a].at[c, j], send_sem=send1.at[3 * a + kk],
                    recv_sem=recv1.at[3 * a + kk], device_id=(x ^ fx, y ^ fy, c), device_id_type=MESH)
                cp.start()
                firsts.append(cp)
        passed = []
        for a in range(n):
            for kk, (fx, fy) in enumerate(CHIP_FLIPS):
                jp = j ^ (2 * fx + fy)
                landed = out[a].at[c, jp]
                pltpu.make_async_remote_copy(
                    src_ref=landed, dst_ref=landed, send_sem=send1.at[3 * a + kk], recv_sem=recv1.at[3 * a + kk],
                    device_id=(x ^ fx, y ^ fy, c), device_id_type=MESH).wait_recv()
                cp = pltpu.make_async_remote_copy(
                    src_ref=landed, dst_ref=landed, send_sem=send2.at[3 * a + kk], recv_sem=recv2.at[3 * a + kk],
                    device_id=(x, y, 1 - c), device_id_type=MESH)
                cp.start()
                passed.append(cp)
        for a in range(n):
            for kk, (fx, fy) in enumerate(CHIP_FLIPS):
                jp = j ^ (2 * fx + fy)
                theirs = out[a].at[1 - c, jp]
                pltpu.make_async_remote_copy(
                    src_ref=theirs, dst_ref=theirs, send_sem=send2.at[3 * a + kk], recv_sem=recv2.at[3 * a + kk],
                    device_id=(x, y, 1 - c), device_id_type=MESH).wait_recv()
        for cp in firsts + passed:
            cp.wait_send()
        for cp in locals_:
            cp.wait()

    out_shape = [jax.ShapeDtypeStruct((2, N_CHIPS, HALF_LAYERS) + s.shape[1:], s.dtype) for s in shards]
    return pl.pallas_call(
        body, name=name, in_specs=_hbm_specs(n), out_specs=_hbm_specs(n), out_shape=out_shape,
        scratch_shapes=[pltpu.SemaphoreType.DMA((3 * n,)), pltpu.SemaphoreType.DMA((3 * n,)),
                        pltpu.SemaphoreType.DMA((3 * n,)), pltpu.SemaphoreType.DMA((3 * n,)),
                        pltpu.SemaphoreType.DMA((2 * n,))])(*shards)


def swap_with_sibling(arrays, name, lead_other_half=False):
    n = len(arrays)

    def body(*refs):
        src = refs[:n]
        out = refs[n:2 * n]
        send, recv = refs[2 * n:]
        x, y, c, _ = _place()
        cps = []
        for a in range(n):
            s_ref = src[a].at[:, pl.ds((1 - c) * HALF_LAYERS, HALF_LAYERS)] if lead_other_half else src[a]
            cp = pltpu.make_async_remote_copy(src_ref=s_ref, dst_ref=out[a], send_sem=send.at[a], recv_sem=recv.at[a],
                                              device_id=(x, y, 1 - c), device_id_type=MESH)
            cp.start()
            cps.append(cp)
        for cp in cps:
            cp.wait()

    if lead_other_half:
        out_shape = [jax.ShapeDtypeStruct((N_CHIPS, HALF_LAYERS) + a.shape[2:], a.dtype) for a in arrays]
    else:
        out_shape = [jax.ShapeDtypeStruct(a.shape, a.dtype) for a in arrays]
    return pl.pallas_call(
        body, name=name, in_specs=_hbm_specs(n), out_specs=_hbm_specs(n), out_shape=out_shape,
        scratch_shapes=[pltpu.SemaphoreType.DMA((n,)), pltpu.SemaphoreType.DMA((n,))])(*arrays)


def share_halves(arrays, name):
    n = len(arrays)

    def body(*refs):
        src = refs[:n]
        out = refs[n:2 * n]
        send, recv, loc = refs[2 * n:]
        x, y, c, _ = _place()
        cps, lcs = [], []
        for a in range(n):
            lc = pltpu.make_async_copy(src[a], out[a].at[c], loc.at[a])
            lc.start()
            lcs.append(lc)
            cp = pltpu.make_async_remote_copy(src_ref=src[a], dst_ref=out[a].at[c], send_sem=send.at[a],
                                              recv_sem=recv.at[a], device_id=(x, y, 1 - c), device_id_type=MESH)
            cp.start()
            cps.append(cp)
        for a in range(n):
            cps[a].wait_send()
            theirs = out[a].at[1 - c]
            pltpu.make_async_remote_copy(src_ref=theirs, dst_ref=theirs, send_sem=send.at[a], recv_sem=recv.at[a],
                                         device_id=(x, y, 1 - c), device_id_type=MESH).wait_recv()
            lcs[a].wait()

    out_shape = [jax.ShapeDtypeStruct((2,) + a.shape, a.dtype) for a in arrays]
    return pl.pallas_call(
        body, name=name, in_specs=_hbm_specs(n), out_specs=_hbm_specs(n), out_shape=out_shape,
        scratch_shapes=[pltpu.SemaphoreType.DMA((n,)), pltpu.SemaphoreType.DMA((n,)),
                        pltpu.SemaphoreType.DMA((n,))])(*arrays)


def swap_rows(arrays, name, axis):
    n = len(arrays)
    count = 2 if axis == "x" else 1

    def body(*refs):
        src = refs[:n]
        out = refs[n:2 * n]
        send, recv = refs[2 * n:]
        x, y, c, _ = _place()
        peer = (1 - x, y, c) if axis == "x" else (x, 1 - y, c)
        start = 2 * (1 - x) if axis == "x" else 1 - y
        cps = []
        for a in range(n):
            cp = pltpu.make_async_remote_copy(src_ref=src[a].at[pl.ds(start, count)], dst_ref=out[a], send_sem=send.at[a],
                                              recv_sem=recv.at[a], device_id=peer, device_id_type=MESH)
            cp.start()
            cps.append(cp)
        for cp in cps:
            cp.wait()

    out_shape = [jax.ShapeDtypeStruct((count,) + a.shape[1:], a.dtype) for a in arrays]
    return pl.pallas_call(
        body, name=name, in_specs=_hbm_specs(n), out_specs=_hbm_specs(n), out_shape=out_shape,
        scratch_shapes=[pltpu.SemaphoreType.DMA((n,)), pltpu.SemaphoreType.DMA((n,))])(*arrays)


def reduce_to_shards(chip_major, names, group):
    x, y, c = lax.axis_index("x"), lax.axis_index("y"), lax.axis_index("c")
    theirs = swap_with_sibling(chip_major, "grad_pair_" + group, lead_other_half=True)
    pair_wire, pair32 = [], []
    for n, g, t in zip(names, chip_major, theirs):
        mine = lax.dynamic_slice_in_dim(g, c * HALF_LAYERS, HALF_LAYERS, axis=1)
        s_wire, s32 = elementwise("pair_sum_" + n, lambda a, b: (a.astype(f32) + b.astype(f32),) * 2, [mine, t],
                                  [mine.dtype, f32])
        pair_wire.append(s_wire)
        pair32.append(s32)
    from_x = swap_rows(pair_wire, "grad_x_" + group, "x")
    col_wire, col32 = [], []
    for n, s32, r in zip(names, pair32, from_x):
        own = lax.dynamic_slice_in_dim(s32, 2 * x, 2, axis=0)
        s_wire, s_32 = elementwise("col_sum_" + n, lambda a, b: (a + b.astype(f32),) * 2, [own, r], [r.dtype, f32])
        col_wire.append(s_wire)
        col32.append(s_32)
    from_y = swap_rows(col_wire, "grad_y_" + group, "y")
    halves = []
    for n, s32, r in zip(names, col32, from_y):
        own = lax.dynamic_index_in_dim(s32, y, axis=0, keepdims=False)
        halves.append(elementwise("chip_sum_" + n, lambda a, b: (a + b.astype(f32),), [own, r[0]], [f32])[0])
    return share_halves(halves, "grad_share_" + group)


def gather_all_devices(vec, name):
    def body(src, out, send, recv, loc):
        x, y, c, _ = _place()
        me = 4 * x + 2 * y + c
        lc = pltpu.make_async_copy(src, out.at[me], loc.at[0])
        lc.start()
        cps = []
        for mask in range(1, 8):
            fx, fy, fc = (mask >> 2) & 1, (mask >> 1) & 1, mask & 1
            cp = pltpu.make_async_remote_copy(src_ref=src, dst_ref=out.at[me], send_sem=send.at[mask - 1],
                                              recv_sem=recv.at[mask - 1], device_id=(x ^ fx, y ^ fy, c ^ fc),
                                              device_id_type=MESH)
            cp.start()
            cps.append(cp)
        for mask in range(1, 8):
            cps[mask - 1].wait_send()
            theirs = out.at[me ^ mask]
            pltpu.make_async_remote_copy(src_ref=theirs, dst_ref=theirs, send_sem=send.at[mask - 1],
                                         recv_sem=recv.at[mask - 1], device_id=(x, y, c), device_id_type=MESH).wait_recv()
        lc.wait()

    return pl.pallas_call(
        body, name=name, in_specs=_hbm_specs(1), out_specs=pl.BlockSpec(memory_space=pl.ANY),
        out_shape=jax.ShapeDtypeStruct((8,) + vec.shape, vec.dtype),
        scratch_shapes=[pltpu.SemaphoreType.DMA((7,)), pltpu.SemaphoreType.DMA((7,)), pltpu.SemaphoreType.DMA((1,))])(vec)


def _to_chip_major(g, axis):
    shp = g.shape
    g = g.reshape(shp[:axis] + (N_CHIPS, shp[axis] // N_CHIPS) + shp[axis + 1:])
    return jnp.moveaxis(g, axis, 0)


def _from_chip_major(b, axis):
    b = jnp.moveaxis(b, 0, axis)
    shp = b.shape
    return b.reshape(shp[:axis] + (shp[axis] * shp[axis + 1],) + shp[axis + 2:])


def _pack(arrays):
    flat = jnp.concatenate([a.reshape(-1) for a in arrays])
    pad = (-flat.shape[0]) % (256 * LANES)
    return jnp.concatenate([flat, jnp.zeros((pad,), flat.dtype)]).reshape(-1, LANES)


def _unpack(packed, like):
    flat = packed.reshape(-1)
    out, off = [], 0
    for a in like:
        out.append(flat[off:off + a.size].reshape(a.shape))
        off += a.size
    return out


def kernel(x, positions, norm_mix, w_in, mla_q_norm, mla_w_uq, mla_kv_norm, mla_w_ukv, lru_conv_w, lru_conv_b, lru_w_a, lru_b_a, lru_w_x, lru_b_x, lru_lambda, gdn_conv_w, gdn_a_log, gdn_dt_bias, gdn_norm, w_branch, b_gate, w_out, norm_ffn, ffn_w_gate, ffn_w_up, ffn_w_down, norm_final, loss_target, m_norm_mix, m_w_in, m_mla_q_norm, m_mla_w_uq, m_mla_kv_norm, m_mla_w_ukv, m_lru_conv_w, m_lru_conv_b, m_lru_w_a, m_lru_b_a, m_lru_w_x, m_lru_b_x, m_lru_lambda, m_gdn_conv_w, m_gdn_a_log, m_gdn_dt_bias, m_gdn_norm, m_w_branch, m_b_gate, m_w_out, m_norm_ffn, m_ffn_w_gate, m_ffn_w_up, m_ffn_w_down, m_norm_final, v_norm_mix, v_w_in, v_mla_q_norm, v_mla_w_uq, v_mla_kv_norm, v_mla_w_ukv, v_lru_conv_w, v_lru_conv_b, v_lru_w_a, v_lru_b_a, v_lru_w_x, v_lru_b_x, v_lru_lambda, v_gdn_conv_w, v_gdn_a_log, v_gdn_dt_bias, v_gdn_norm, v_w_branch, v_b_gate, v_w_out, v_norm_ffn, v_ffn_w_gate, v_ffn_w_up, v_ffn_w_down, v_norm_final):
    given = dict(locals())
    w = {n: given[n] for n in WEIGHTS}
    m = {n: given["m_" + n] for n in WEIGHTS}
    v = {n: given["v_" + n] for n in WEIGHTS}

    wire = {n: (w[n] if n in WIRE_F32 else elementwise("cast_" + n, lambda a: (a,), [w[n]], [bf16])[0]) for n in SHARDED}
    big = [n for n in SHARDED if n not in WIRE_F32]
    gathered = dict(zip(big, gather_weights([wire[n] for n in big], "gather_big")))
    gathered.update(zip(WIRE_F32, gather_weights([wire[n] for n in WIRE_F32], "gather_small")))
    layers = []
    for li in range(DEPTH):
        p = {n: _from_chip_major(gathered[n][li // HALF_LAYERS, :, li % HALF_LAYERS], SHARD_AXIS[n]) for n in SHARDED}
        p.update({n: w[n][li] for n in REPLICATED if n != "norm_final"})
        layers.append(p)

    loss, grad_x, d_layers, d_final = local_step(layers, norm_final, x[0], positions[0], loss_target[0])
    loss = lax.psum(loss, ("x", "y", "c"))

    grads = {}
    chip_major = {n: jnp.stack([_to_chip_major(d_layers[li][n], SHARD_AXIS[n]) for li in range(DEPTH)], axis=1)
                  for n in SHARDED}
    for group, names in (("big", big), ("small", list(WIRE_F32))):
        for n, both in zip(names, reduce_to_shards([chip_major[n] for n in names], names, group)):
            grads[n] = both.reshape(w[n].shape)

    rep_g = [jnp.stack([d_layers[li][n] for li in range(DEPTH)]) for n in REPLICATED if n != "norm_final"] + [d_final]
    everyone = gather_all_devices(_pack(rep_g), "grad_replicated")
    total = elementwise("replicated_sum", lambda *a: (functools.reduce(lambda p, q: p + q, a),),
                        [everyone[d] for d in range(8)], [f32])[0]
    grads.update(zip(REPLICATED, _unpack(total, [w[n] for n in REPLICATED])))

    delta, new_m, new_v = {}, {}, {}
    for n in SHARDED:
        delta[n], new_m[n], new_v[n] = elementwise("adamw_" + n, _adamw, [w[n], grads[n], m[n], v[n]], [f32] * 3)
    rep = [n for n in REPLICATED]
    pd, pm, pv = elementwise("adamw_replicated", _adamw,
                             [_pack([w[n] for n in rep]), total, _pack([m[n] for n in rep]), _pack([v[n] for n in rep])],
                             [f32] * 3)
    for dst, packed in ((delta, pd), (new_m, pm), (new_v, pv)):
        dst.update(zip(rep, _unpack(packed, [w[n] for n in rep])))

    return (loss, grad_x[None], *[grads[n] for n in WEIGHTS], *[delta[n] for n in WEIGHTS],
            *[new_m[n] for n in WEIGHTS], *[new_v[n] for n in WEIGHTS])
```

```python
import functools
import math

import jax
import jax.numpy as jnp
from jax import lax
from jax.experimental import pallas as pl
from jax.experimental.pallas import tpu as pltpu

f32 = jnp.float32
bf16 = jnp.bfloat16
MESH = pl.DeviceIdType.MESH

D_MODEL = 2048
DEPTH = 4
EPS = 1e-6
N_BRANCH = 3
BRANCH_WIDTH = 1024
MLA_HEADS = 8
QK_NOPE = 128
QK_ROPE = 64
V_HEAD = 128
Q_LORA = 512
KV_LORA = 512
ROPE_THETA = 10000.0
LRU_WIDTH = 1024
LRU_BLOCKS = 8
LRU_BLOCK_W = 128
LRU_C = 8.0
CONV_W = 4
GDN_HEADS = 8
GDN_DK = 128
GDN_DV = 128
GDN_CHUNK = 64
FFN_HIDDEN = 5632
QK_PAD = 256
ATT_SCALE = (QK_NOPE + QK_ROPE) ** -0.5

ADAM_LR = 0.001
ADAM_B1 = 0.9
ADAM_B2 = 0.999
ADAM_EPS = 1e-08
ADAM_WD = 0.01
ADAM_STEP = 10

N_CHIPS = 4
HALF_LAYERS = DEPTH // 2
LANES = 128
SUBLANES = 8
NEG_BIG = -1e30

SHARDED = ["w_in", "mla_w_uq", "mla_w_ukv", "lru_conv_w", "gdn_conv_w", "w_branch", "b_gate", "w_out",
           "ffn_w_gate", "ffn_w_up", "ffn_w_down"]
SHARD_AXIS = {"w_in": 1, "mla_w_uq": 1, "mla_w_ukv": 1, "lru_conv_w": 1, "gdn_conv_w": 1, "w_branch": 2,
              "b_gate": 1, "w_out": 0, "ffn_w_gate": 1, "ffn_w_up": 1, "ffn_w_down": 0}
WIRE_F32 = ("lru_conv_w", "gdn_conv_w", "b_gate")
REPLICATED = ["norm_mix", "mla_q_norm", "mla_kv_norm", "lru_conv_b", "lru_w_a", "lru_b_a", "lru_w_x", "lru_b_x",
              "lru_lambda", "gdn_a_log", "gdn_dt_bias", "gdn_norm", "norm_ffn", "norm_final"]
WEIGHTS = ["norm_mix", "w_in", "mla_q_norm", "mla_w_uq", "mla_kv_norm", "mla_w_ukv", "lru_conv_w", "lru_conv_b",
           "lru_w_a", "lru_b_a", "lru_w_x", "lru_b_x", "lru_lambda", "gdn_conv_w", "gdn_a_log", "gdn_dt_bias",
           "gdn_norm", "w_branch", "b_gate", "w_out", "norm_ffn", "ffn_w_gate", "ffn_w_up", "ffn_w_down",
           "norm_final"]

IN_PIECES = [("c_q", 512), ("c_kv", 512), ("kra", QK_PAD), ("krb", QK_PAD), ("lru_x", 1024), ("lru_y", 1024),
             ("g_q", 1024), ("g_k", 1024), ("g_v", 1024), ("g_z", 1024), ("g_ba", 128),
             ("gl0", D_MODEL), ("gl1", D_MODEL), ("gl2", D_MODEL), ("pad", 128)]
IN_EXT = sum(w for _, w in IN_PIECES)


def _dot(a, b, dims, precision=None):
    return lax.dot_general(a, b, (dims, ((), ())), precision=precision, preferred_element_type=f32)


NN = ((1,), (0,))
NT = ((1,), (1,))
TN = ((0,), (0,))
HI = lax.Precision.HIGHEST
MID = lax.Precision.HIGH


def _bdot(a, b, dims):
    return _dot(a.astype(bf16), b.astype(bf16), dims)


def _largest_tile(n, cap, quantum):
    if n <= cap:
        return n
    best = None
    for t in range(quantum, cap + 1, quantum):
        if n % t == 0:
            best = t
    assert best is not None, (n, cap, quantum)
    return best


def _params(sem):
    return pltpu.CompilerParams(dimension_semantics=sem)


def _matmul(a, b, *, dims, out_dtype, name):
    if dims == TN:
        kdim, m = a.shape
        kdim2, n = b.shape
    elif dims == NT:
        m, kdim = a.shape
        n, kdim2 = b.shape
    else:
        m, kdim = a.shape
        kdim2, n = b.shape
    assert kdim == kdim2, (a.shape, b.shape, dims)
    tm = _largest_tile(m, 1024, LANES)
    tn = _largest_tile(n, 1536, LANES)
    tk = _largest_tile(kdim, 1536, LANES)
    nk = kdim // tk

    def body(a_ref, b_ref, o_ref, acc_ref):
        k = pl.program_id(2)

        @pl.when(k == 0)
        def _():
            acc_ref[...] = jnp.zeros_like(acc_ref)

        acc_ref[...] += _dot(a_ref[...].astype(bf16), b_ref[...].astype(bf16), dims)

        @pl.when(k == nk - 1)
        def _():
            o_ref[...] = acc_ref[...].astype(o_ref.dtype)

    if dims == TN:
        a_spec = pl.BlockSpec((tk, tm), lambda i, j, k: (k, i))
        b_spec = pl.BlockSpec((tk, tn), lambda i, j, k: (k, j))
    elif dims == NT:
        a_spec = pl.BlockSpec((tm, tk), lambda i, j, k: (i, k))
        b_spec = pl.BlockSpec((tn, tk), lambda i, j, k: (j, k))
    else:
        a_spec = pl.BlockSpec((tm, tk), lambda i, j, k: (i, k))
        b_spec = pl.BlockSpec((tk, tn), lambda i, j, k: (k, j))
    return pl.pallas_call(
        body, name=name, grid=(m // tm, n // tn, nk), in_specs=[a_spec, b_spec],
        out_specs=pl.BlockSpec((tm, tn), lambda i, j, k: (i, j)),
        out_shape=jax.ShapeDtypeStruct((m, n), out_dtype),
        scratch_shapes=[pltpu.VMEM((tm, tn), f32)],
        compiler_params=_params(("parallel", "parallel", "arbitrary")))(a, b)


def linear(a, w, name):
    @jax.custom_vjp
    def op(a, w):
        return _matmul(a, w, dims=NN, out_dtype=f32, name=name + "_fwd")

    def fwd(a, w):
        return op(a, w), (a, w)

    def bwd(res, dy):
        a, w = res
        da = _matmul(dy, w, dims=NT, out_dtype=f32, name=name + "_da")
        dw = _matmul(a, dy, dims=TN, out_dtype=w.dtype, name=name + "_dw")
        return da, dw

    op.defvjp(fwd, bwd)
    return op(a, w)


def _row_tile(s, widths):
    budget = 12 * 1024 * 1024
    t = 512
    while t > SUBLANES and 2 * 4 * t * sum(widths) > budget:
        t //= 2
    return min(t, s)


def row_op(name, fn, rows, consts, outs, groups=1):
    s = rows[0][0].shape[0]
    r_arrs = [a for a, _ in rows]
    r_kinds = [k for _, k in rows]
    c_arrs = [a for a, _ in consts]
    c_kinds = [k for _, k in consts]
    r_w = [a.shape[1] // groups if k[0] == "g" else a.shape[1] for a, k in rows]
    o_w = [w for w, _ in outs]
    n_r, n_c, n_o = len(rows), len(consts), len(outs)
    diff_r = [i for i, k in enumerate(r_kinds) if not k.endswith("n")]

    def r_spec(i, tile):
        if r_kinds[i][0] == "g":
            return pl.BlockSpec((tile, r_w[i]), lambda r, g: (r, g))
        return pl.BlockSpec((tile, r_w[i]), lambda r, g: (r, 0))

    def c_spec(i):
        nd = c_arrs[i].ndim
        return pl.BlockSpec(c_arrs[i].shape, lambda r, g, nd=nd: (0,) * nd)

    def o_spec(i, tile):
        return pl.BlockSpec((tile, o_w[i]), lambda r, g: (r, g))

    def c_val(ref, kind, g):
        return ref[g] if kind == "p" else ref[...]

    def run_fwd(r_vals, c_vals):
        tile = _row_tile(s, r_w + o_w)

        def body(*refs):
            g = pl.program_id(1)
            rv = [refs[i][...] for i in range(n_r)]
            cv = [c_val(refs[n_r + i], c_kinds[i], g) for i in range(n_c)]
            res = fn(*rv, *cv)
            for i in range(n_o):
                refs[n_r + n_c + i][...] = res[i].astype(outs[i][1])

        res = pl.pallas_call(
            body, name=name + "_fwd", grid=(s // tile, groups),
            in_specs=[r_spec(i, tile) for i in range(n_r)] + [c_spec(i) for i in range(n_c)],
            out_specs=[o_spec(i, tile) for i in range(n_o)],
            out_shape=[jax.ShapeDtypeStruct((s, groups * o_w[i]), outs[i][1]) for i in range(n_o)],
            compiler_params=_params(("parallel", "arbitrary")))(*r_vals, *c_vals)
        return tuple(res)

    def run_bwd(r_vals, c_vals, cts):
        tile = _row_tile(s, r_w + o_w + o_w + [r_w[i] for i in diff_r])

        def body(*refs):
            r = pl.program_id(0)
            g = pl.program_id(1)
            in_refs = refs[:n_r + n_c + n_o]
            dr_refs = refs[n_r + n_c + n_o:n_r + n_c + n_o + len(diff_r)]
            dc_refs = refs[n_r + n_c + n_o + len(diff_r):]
            rv = [in_refs[i][...] for i in range(n_r)]
            cv = [c_val(in_refs[n_r + i], c_kinds[i], g) for i in range(n_c)]
            ct = tuple(in_refs[n_r + n_c + i][...].astype(f32) for i in range(n_o))

            @pl.when((r == 0) & (g == 0))
            def _():
                for d in dc_refs:
                    d[...] = jnp.zeros_like(d)

            def f(*dv):
                full = list(rv)
                for j, i in enumerate(diff_r):
                    full[i] = dv[j]
                return tuple(o.astype(f32) for o in fn(*full, *dv[len(diff_r):]))

            _, vjp = jax.vjp(f, *[rv[i] for i in diff_r], *cv)
            grads = vjp(ct)
            for j, i in enumerate(diff_r):
                if r_kinds[i][0] == "g" or groups == 1:
                    dr_refs[j][...] = grads[j]
                else:
                    @pl.when(g == 0)
                    def _(j=j):
                        dr_refs[j][...] = grads[j]

                    @pl.when(g > 0)
                    def _(j=j):
                        dr_refs[j][...] += grads[j]
            for i in range(n_c):
                gc = grads[len(diff_r) + i]
                if c_kinds[i] == "p":
                    dc_refs[i][g] += gc
                else:
                    dc_refs[i][...] += gc

        res = pl.pallas_call(
            body, name=name + "_bwd", grid=(s // tile, groups),
            in_specs=[r_spec(i, tile) for i in range(n_r)] + [c_spec(i) for i in range(n_c)]
            + [o_spec(i, tile) for i in range(n_o)],
            out_specs=[r_spec(i, tile) for i in diff_r] + [c_spec(i) for i in range(n_c)],
            out_shape=[jax.ShapeDtypeStruct(r_arrs[i].shape, f32) for i in diff_r]
            + [jax.ShapeDtypeStruct(c.shape, f32) for c in c_arrs],
            compiler_params=_params(("arbitrary", "arbitrary")))(*r_vals, *c_vals, *cts)
        return res[:len(diff_r)], res[len(diff_r):]

    @jax.custom_vjp
    def op(r_vals, c_vals):
        return run_fwd(r_vals, c_vals)

    def fwd(r_vals, c_vals):
        return run_fwd(r_vals, c_vals), (r_vals, c_vals)

    def bwd(res, cts):
        r_vals, c_vals = res
        d_r, d_c = run_bwd(r_vals, c_vals, cts)
        full = [jnp.zeros_like(v) for v in r_vals]
        for j, i in enumerate(diff_r):
            full[i] = d_r[j]
        return tuple(full), tuple(d_c)

    op.defvjp(fwd, bwd)
    return op(tuple(r_arrs), tuple(c_arrs))


def _rms(x, w):
    return x * lax.rsqrt(jnp.mean(x * x, axis=-1, keepdims=True) + EPS) * w


def rms_op(x, w, name, groups=1):
    return row_op(name, lambda x, w: (_rms(x, w),), [(x, "g")], [(w.reshape(1, -1), "c")],
                  [(x.shape[1] // groups, f32)], groups)[0]


def split_cols(x, widths):
    offs = [0]
    for w in widths:
        offs.append(offs[-1] + w)

    @jax.custom_vjp
    def op(x):
        return tuple(x[:, offs[i]:offs[i + 1]] for i in range(len(widths)))

    def fwd(x):
        return op(x), None

    def bwd(_, cts):
        return (jnp.concatenate(cts, axis=1),)

    op.defvjp(fwd, bwd)
    return op(x)


def rope_tables(positions):
    s = positions.shape[0]
    tile = min(s, 512)
    half = QK_ROPE // 2

    def body(p_ref, c_ref, s_ref):
        pos = p_ref[...].astype(f32)
        lane = lax.broadcasted_iota(jnp.int32, (1, QK_PAD), 1)
        idx = ((lane - QK_NOPE) % half).astype(f32)
        inv = jnp.exp(idx * (-math.log(ROPE_THETA) / half))
        ang = pos * inv
        rot = (lane >= QK_NOPE) & (lane < QK_NOPE + QK_ROPE)
        c_ref[...] = jnp.where(rot, jnp.cos(ang), jnp.where(lane < QK_NOPE, 1.0, 0.0))
        s_ref[...] = jnp.where(rot, jnp.sin(ang), 0.0)

    return pl.pallas_call(
        body, name="rope_tables", grid=(s // tile,), in_specs=[pl.BlockSpec((tile, 1), lambda i: (i, 0))],
        out_specs=[pl.BlockSpec((tile, QK_PAD), lambda i: (i, 0))] * 2,
        out_shape=[jax.ShapeDtypeStruct((s, QK_PAD), f32)] * 2,
        compiler_params=_params(("parallel",)))(positions)


ATT_ROW_GROUPS = 4


def _att_tile(s):
    return min(s, 1024)


def _att_fwd(q, k, v):
    s = q.shape[0]
    t = _att_tile(s)
    n = s // t

    qi_tab, kj_tab = _block_pairs(n, by_query=True)

    def body(qi_ref, kj_ref, q_ref, k_ref, v_ref, o_ref, lse_ref, m_sc, l_sc, acc_sc):
        pair = pl.program_id(1)
        qi = qi_ref[pair]
        kj = kj_ref[pair]

        @pl.when(kj == 0)
        def _():
            m_sc[...] = jnp.full_like(m_sc, NEG_BIG)
            l_sc[...] = jnp.zeros_like(l_sc)
            acc_sc[...] = jnp.zeros_like(acc_sc)

        def accumulate(diagonal):
            for rows in _row_groups(t):
                sc = _dot(q_ref[rows, :], k_ref[...], NT) * ATT_SCALE
                if diagonal:
                    sc = _mask_diagonal(sc, rows.start)
                m_old = m_sc[rows, :]
                m_new = jnp.maximum(m_old, jnp.max(sc, axis=1, keepdims=True))
                alpha = jnp.exp(m_old - m_new)
                p = jnp.exp(sc - m_new)
                l_sc[rows, :] = alpha * l_sc[rows, :] + jnp.sum(p, axis=1, keepdims=True)
                acc_sc[rows, :] = alpha * acc_sc[rows, :] + _dot(p.astype(bf16), v_ref[...], NN)
                m_sc[rows, :] = m_new

        @pl.when(kj < qi)
        def _():
            accumulate(False)

        @pl.when(kj == qi)
        def _():
            accumulate(True)
            o_ref[...] = acc_sc[...] / l_sc[...]
            lse_ref[...] = jnp.broadcast_to(m_sc[...] + jnp.log(l_sc[...]), (t, LANES))

    qmap = lambda h, p, qi_ref, kj_ref: (qi_ref[p], h)
    kmap = lambda h, p, qi_ref, kj_ref: (kj_ref[p], h)
    return pl.pallas_call(
        body, name="mla_att_fwd",
        grid_spec=pltpu.PrefetchScalarGridSpec(
            num_scalar_prefetch=2, grid=(MLA_HEADS, len(qi_tab)),
            in_specs=[pl.BlockSpec((t, QK_PAD), qmap), pl.BlockSpec((t, QK_PAD), kmap),
                      pl.BlockSpec((t, V_HEAD), kmap)],
            out_specs=[pl.BlockSpec((t, V_HEAD), qmap), pl.BlockSpec((t, LANES), qmap)],
            scratch_shapes=[pltpu.VMEM((t, 1), f32), pltpu.VMEM((t, 1), f32), pltpu.VMEM((t, V_HEAD), f32)]),
        out_shape=[jax.ShapeDtypeStruct((s, MLA_HEADS * V_HEAD), f32),
                   jax.ShapeDtypeStruct((s, MLA_HEADS * LANES), f32)],
        compiler_params=_params(("parallel", "arbitrary")))(qi_tab, kj_tab, q, k, v)


def _block_pairs(n, by_query):
    if by_query:
        pairs = [(i, j) for i in range(n) for j in range(i + 1)]
    else:
        pairs = [(i, j) for j in range(n) for i in range(j, n)]
    return (jnp.asarray([p[0] for p in pairs], jnp.int32), jnp.asarray([p[1] for p in pairs], jnp.int32))


def _mask_diagonal(sc, row0):
    rows = row0 + lax.broadcasted_iota(jnp.int32, sc.shape, 0)
    cols = lax.broadcasted_iota(jnp.int32, sc.shape, 1)
    return jnp.where(cols <= rows, sc, NEG_BIG)


def _row_groups(t):
    return [slice(rs, rs + t // ATT_ROW_GROUPS) for rs in range(0, t, t // ATT_ROW_GROUPS)]


def _att_probs(qs, k, lses, dos, os_, v, groups, diagonal):
    scs = [_dot(q, k, NT) * ATT_SCALE for q in qs]
    if diagonal:
        scs = [_mask_diagonal(sc, rows.start) for sc, rows in zip(scs, groups)]
    ps = _each(lambda sc, lse: jnp.exp(sc - lse[:, :1]), scs, lses)
    dps = [_dot(do, v, NT) for do in dos]
    deltas = _each(lambda do, o: jnp.sum(do.astype(f32) * o, axis=1, keepdims=True), dos, os_)
    dss = _each(lambda p, dp, delta: p * (dp - delta), ps, dps, deltas)
    return ps, dss


def _att_bwd_kv(q, k, v, o, lse, do):
    s = q.shape[0]
    t = _att_tile(s)
    n = s // t

    qi_tab, kj_tab = _block_pairs(n, by_query=False)

    def body(qi_ref, kj_ref, q_ref, k_ref, v_ref, o_ref, lse_ref, do_ref, dk_ref, dv_ref, dk_sc, dv_sc):
        pair = pl.program_id(1)
        qi = qi_ref[pair]
        kj = kj_ref[pair]

        def accumulate(diagonal):
            dv = dk = None
            for rows in _row_groups(t):
                (p,), (ds,) = _att_probs([q_ref[rows, :]], k_ref[...], [lse_ref[rows, :]], [do_ref[rows, :]],
                                         [o_ref[rows, :]], v_ref[...], [rows], diagonal)
                dv_g = _dot(p.astype(bf16), do_ref[rows, :], TN)
                dk_g = _dot(ds.astype(bf16), q_ref[rows, :], TN)
                dv = dv_g if dv is None else dv + dv_g
                dk = dk_g if dk is None else dk + dk_g
            return dv, dk

        @pl.when(qi == kj)
        def _():
            dv, dk = accumulate(True)
            dv_sc[...] = dv
            dk_sc[...] = dk

        @pl.when(qi > kj)
        def _():
            dv, dk = accumulate(False)
            dv_sc[...] += dv
            dk_sc[...] += dk

        @pl.when(qi == n - 1)
        def _():
            dk_ref[...] = dk_sc[...] * ATT_SCALE
            dv_ref[...] = dv_sc[...]

    qmap = lambda h, p, qi_ref, kj_ref: (qi_ref[p], h)
    kmap = lambda h, p, qi_ref, kj_ref: (kj_ref[p], h)
    return pl.pallas_call(
        body, name="mla_att_bwd_kv",
        grid_spec=pltpu.PrefetchScalarGridSpec(
            num_scalar_prefetch=2, grid=(MLA_HEADS, len(qi_tab)),
            in_specs=[pl.BlockSpec((t, QK_PAD), qmap), pl.BlockSpec((t, QK_PAD), kmap), pl.BlockSpec((t, V_HEAD), kmap),
                      pl.BlockSpec((t, V_HEAD), qmap), pl.BlockSpec((t, LANES), qmap), pl.BlockSpec((t, V_HEAD), qmap)],
            out_specs=[pl.BlockSpec((t, QK_PAD), kmap), pl.BlockSpec((t, V_HEAD), kmap)],
            scratch_shapes=[pltpu.VMEM((t, QK_PAD), f32), pltpu.VMEM((t, V_HEAD), f32)]),
        out_shape=[jax.ShapeDtypeStruct((s, MLA_HEADS * QK_PAD), f32), jax.ShapeDtypeStruct((s, MLA_HEADS * V_HEAD), f32)],
        compiler_params=_params(("parallel", "arbitrary")))(qi_tab, kj_tab, q, k, v, o, lse, do)


def _att_bwd_q(q, k, v, o, lse, do):
    s = q.shape[0]
    t = _att_tile(s)
    n = s // t

    qi_tab, kj_tab = _block_pairs(n, by_query=True)

    def body(qi_ref, kj_ref, q_ref, k_ref, v_ref, o_ref, lse_ref, do_ref, dq_ref, dq_sc):
        pair = pl.program_id(1)
        qi = qi_ref[pair]
        kj = kj_ref[pair]

        @pl.when(kj == 0)
        def _():
            dq_sc[...] = jnp.zeros_like(dq_sc)

        def accumulate(diagonal):
            for rows in _row_groups(t):
                _, (ds,) = _att_probs([q_ref[rows, :]], k_ref[...], [lse_ref[rows, :]], [do_ref[rows, :]],
                                      [o_ref[rows, :]], v_ref[...], [rows], diagonal)
                dq_sc[rows, :] += _dot(ds.astype(bf16), k_ref[...], NN)

        @pl.when(kj < qi)
        def _():
            accumulate(False)

        @pl.when(kj == qi)
        def _():
            accumulate(True)
            dq_ref[...] = dq_sc[...] * ATT_SCALE

    qmap = lambda h, p, qi_ref, kj_ref: (qi_ref[p], h)
    kmap = lambda h, p, qi_ref, kj_ref: (kj_ref[p], h)
    return pl.pallas_call(
        body, name="mla_att_bwd_q",
        grid_spec=pltpu.PrefetchScalarGridSpec(
            num_scalar_prefetch=2, grid=(MLA_HEADS, len(qi_tab)),
            in_specs=[pl.BlockSpec((t, QK_PAD), qmap), pl.BlockSpec((t, QK_PAD), kmap), pl.BlockSpec((t, V_HEAD), kmap),
                      pl.BlockSpec((t, V_HEAD), qmap), pl.BlockSpec((t, LANES), qmap), pl.BlockSpec((t, V_HEAD), qmap)],
            out_specs=pl.BlockSpec((t, QK_PAD), qmap),
            scratch_shapes=[pltpu.VMEM((t, QK_PAD), f32)]),
        out_shape=jax.ShapeDtypeStruct((s, MLA_HEADS * QK_PAD), f32),
        compiler_params=_params(("parallel", "arbitrary")))(qi_tab, kj_tab, q, k, v, o, lse, do)


@jax.custom_vjp
def attention(q, k, v):
    return _att_fwd(q.astype(bf16), k.astype(bf16), v.astype(bf16))[0]


def _attention_fwd(q, k, v):
    qb, kb, vb = q.astype(bf16), k.astype(bf16), v.astype(bf16)
    o, lse = _att_fwd(qb, kb, vb)
    return o, (qb, kb, vb, o, lse)


def _attention_bwd(res, do):
    qb, kb, vb, o, lse = res
    dob = do.astype(bf16)
    dk, dv = _att_bwd_kv(qb, kb, vb, o, lse, dob)
    dq = _att_bwd_q(qb, kb, vb, o, lse, dob)
    return dq, dk, dv


attention.defvjp(_attention_fwd, _attention_bwd)


def _seq_tile(s):
    return min(s, 256)


def _chan_tile(c):
    return _largest_tile(c, 512, LANES)


def _shift_down(ext, sh, t):
    if sh == 0:
        return ext[SUBLANES:]
    return pltpu.roll(ext, sh, axis=0)[SUBLANES:]


def _conv_fwd_call(x, w, b, name):
    s, c = x.shape
    t, cb = _seq_tile(s), _chan_tile(c)
    hb = t // SUBLANES

    def body(x_ref, xp_ref, w_ref, b_ref, y_ref):
        ti = pl.program_id(1)
        prev = jnp.where(ti == 0, 0.0, xp_ref[...])
        ext = jnp.concatenate([prev, x_ref[...]], axis=0)
        acc = jnp.broadcast_to(b_ref[...], (t, cb))
        for kk in range(CONV_W):
            acc = acc + w_ref[kk:kk + 1, :] * _shift_down(ext, CONV_W - 1 - kk, t)
        y_ref[...] = acc

    return pl.pallas_call(
        body, name=name + "_fwd", grid=(c // cb, s // t),
        in_specs=[pl.BlockSpec((t, cb), lambda ci, ti: (ti, ci)),
                  pl.BlockSpec((SUBLANES, cb), lambda ci, ti: (jnp.maximum(ti * hb - 1, 0), ci)),
                  pl.BlockSpec((CONV_W, cb), lambda ci, ti: (0, ci)),
                  pl.BlockSpec((1, cb), lambda ci, ti: (0, ci))],
        out_specs=pl.BlockSpec((t, cb), lambda ci, ti: (ti, ci)),
        out_shape=jax.ShapeDtypeStruct((s, c), f32),
        compiler_params=_params(("parallel", "arbitrary")))(x, x, w, b)


def _conv_bwd_call(x, w, dy, name):
    s, c = x.shape
    t, cb = _seq_tile(s), _chan_tile(c)
    hb = t // SUBLANES
    nt = s // t

    def body(x_ref, xp_ref, w_ref, dy_ref, dyn_ref, dx_ref, dw_ref, db_ref):
        ti = pl.program_id(1)

        @pl.when(ti == 0)
        def _():
            dw_ref[...] = jnp.zeros_like(dw_ref)
            db_ref[...] = jnp.zeros_like(db_ref)

        dy = dy_ref[...]
        nxt = jnp.where(ti == nt - 1, 0.0, dyn_ref[...])
        dext = jnp.concatenate([dy, nxt], axis=0)
        prev = jnp.where(ti == 0, 0.0, xp_ref[...])
        xext = jnp.concatenate([prev, x_ref[...]], axis=0)
        dx = jnp.zeros((t, cb), f32)
        for kk in range(CONV_W):
            sh = CONV_W - 1 - kk
            up = dext[:t] if sh == 0 else pltpu.roll(dext, t + SUBLANES - sh, axis=0)[:t]
            dx = dx + w_ref[kk:kk + 1, :] * up
            dw_ref[kk:kk + 1, :] += jnp.sum(dy * _shift_down(xext, sh, t), axis=0, keepdims=True)
        dx_ref[...] = dx
        db_ref[...] += jnp.sum(dy, axis=0, keepdims=True)

    return pl.pallas_call(
        body, name=name + "_bwd", grid=(c // cb, nt),
        in_specs=[pl.BlockSpec((t, cb), lambda ci, ti: (ti, ci)),
                  pl.BlockSpec((SUBLANES, cb), lambda ci, ti: (jnp.maximum(ti * hb - 1, 0), ci)),
                  pl.BlockSpec((CONV_W, cb), lambda ci, ti: (0, ci)),
                  pl.BlockSpec((t, cb), lambda ci, ti: (ti, ci)),
                  pl.BlockSpec((SUBLANES, cb), lambda ci, ti: (jnp.minimum((ti + 1) * hb, nt * hb - 1), ci))],
        out_specs=[pl.BlockSpec((t, cb), lambda ci, ti: (ti, ci)),
                   pl.BlockSpec((CONV_W, cb), lambda ci, ti: (0, ci)),
                   pl.BlockSpec((1, cb), lambda ci, ti: (0, ci))],
        out_shape=[jax.ShapeDtypeStruct((s, c), f32), jax.ShapeDtypeStruct((CONV_W, c), f32),
                   jax.ShapeDtypeStruct((1, c), f32)],
        compiler_params=_params(("parallel", "arbitrary")))(x, x, w, dy, dy)


def causal_conv(x, w, b, name):
    @jax.custom_vjp
    def op(x, w, b):
        return _conv_fwd_call(x, w, b, name)

    def fwd(x, w, b):
        return op(x, w, b), (x, w)

    def bwd(res, dy):
        x, w = res
        dx, dw, db = _conv_bwd_call(x, w, dy, name)
        return dx, dw, db

    op.defvjp(fwd, bwd)
    return op(x, w, b)


def _scan_fwd_call(a, b):
    s, c = a.shape
    t, cb = _seq_tile(s), _chan_tile(c)

    def body(a_ref, b_ref, h_ref, carry):
        ti = pl.program_id(1)

        @pl.when(ti == 0)
        def _():
            carry[...] = jnp.zeros_like(carry)

        av, bv = a_ref[...], b_ref[...]
        rows = lax.broadcasted_iota(jnp.int32, (t, cb), 0)
        d = 1
        while d < t:
            a_sh = jnp.where(rows >= d, pltpu.roll(av, d, axis=0), 1.0)
            b_sh = jnp.where(rows >= d, pltpu.roll(bv, d, axis=0), 0.0)
            bv = av * b_sh + bv
            av = av * a_sh
            d *= 2
        h = av * carry[0:1, :] + bv
        h_ref[...] = h
        carry[0:1, :] = h[t - 1:t, :]

    return pl.pallas_call(
        body, name="lru_scan_fwd", grid=(c // cb, s // t),
        in_specs=[pl.BlockSpec((t, cb), lambda ci, ti: (ti, ci))] * 2,
        out_specs=pl.BlockSpec((t, cb), lambda ci, ti: (ti, ci)),
        out_shape=jax.ShapeDtypeStruct((s, c), f32),
        scratch_shapes=[pltpu.VMEM((SUBLANES, cb), f32)],
        compiler_params=_params(("parallel", "arbitrary")))(a, b)


def _scan_bwd_call(a, h, dh):
    s, c = a.shape
    t, cb = _seq_tile(s), _chan_tile(c)
    hb = t // SUBLANES
    nt = s // t

    def body(a_ref, an_ref, h_ref, hp_ref, dh_ref, da_ref, db_ref, carry):
        step = pl.program_id(1)
        ti = nt - 1 - step

        @pl.when(step == 0)
        def _():
            carry[...] = jnp.zeros_like(carry)

        rows = lax.broadcasted_iota(jnp.int32, (t, cb), 0)
        av = a_ref[...]
        an = jnp.where(rows == t - 1, an_ref[0:1, :], pltpu.roll(av, t - 1, axis=0))
        gv = dh_ref[...]
        d = 1
        while d < t:
            a_sh = jnp.where(rows < t - d, pltpu.roll(an, t - d, axis=0), 1.0)
            g_sh = jnp.where(rows < t - d, pltpu.roll(gv, t - d, axis=0), 0.0)
            gv = an * g_sh + gv
            an = an * a_sh
            d *= 2
        g = an * carry[0:1, :] + gv
        carry[0:1, :] = g[0:1, :]
        hv = h_ref[...]
        first = jnp.where(ti == 0, 0.0, hp_ref[SUBLANES - 1:SUBLANES, :])
        h_prev = jnp.where(rows == 0, first, pltpu.roll(hv, 1, axis=0))
        da_ref[...] = g * h_prev
        db_ref[...] = g

    cur = lambda ci, st: (nt - 1 - st, ci)
    return pl.pallas_call(
        body, name="lru_scan_bwd", grid=(c // cb, nt),
        in_specs=[pl.BlockSpec((t, cb), cur),
                  pl.BlockSpec((SUBLANES, cb), lambda ci, st: (jnp.minimum((nt - st) * hb, nt * hb - 1), ci)),
                  pl.BlockSpec((t, cb), cur),
                  pl.BlockSpec((SUBLANES, cb), lambda ci, st: (jnp.maximum((nt - 1 - st) * hb - 1, 0), ci)),
                  pl.BlockSpec((t, cb), cur)],
        out_specs=[pl.BlockSpec((t, cb), cur)] * 2,
        out_shape=[jax.ShapeDtypeStruct((s, c), f32)] * 2,
        scratch_shapes=[pltpu.VMEM((SUBLANES, cb), f32)],
        compiler_params=_params(("parallel", "arbitrary")))(a, a, h, h, dh)


@jax.custom_vjp
def lru_scan(a, b):
    return _scan_fwd_call(a, b)


def _lru_scan_fwd(a, b):
    h = _scan_fwd_call(a, b)
    return h, (a, h)


def _lru_scan_bwd(res, dh):
    a, h = res
    da, db = _scan_bwd_call(a, h, dh)
    return da, db


lru_scan.defvjp(_lru_scan_fwd, _lru_scan_bwd)


def _expm1(x):
    small = x * (1.0 + x / 2.0 * (1.0 + x / 3.0 * (1.0 + x / 4.0 * (1.0 + x / 5.0 * (1.0 + x / 6.0 * (1.0 + x / 7.0))))))
    return jnp.where(jnp.abs(x) < 0.25, small, jnp.exp(x) - 1.0)


def _lru_gates(xc, w_a, b_a, w_x, b_x, lam):
    xb = xc.astype(bf16)
    r = jax.nn.sigmoid(_dot(xb, w_a.astype(bf16), NN) + b_a)
    gi = jax.nn.sigmoid(_dot(xb, w_x.astype(bf16), NN) + b_x)
    log_a = -LRU_C * r * jax.nn.softplus(-lam)
    a = jnp.exp(log_a)
    mult = jnp.sqrt(-_expm1(2.0 * log_a))
    return a, mult * (gi * xc)


def _each(fn, *lists):
    return [fn(*items) for items in zip(*lists)]


@jax.custom_vjp
def _inv_unit_lower(ls):
    c = ls[0].shape[0]
    eye = (lax.broadcasted_iota(jnp.int32, (c, c), 0) == lax.broadcasted_iota(jnp.int32, (c, c), 1)).astype(f32)
    ps = [eye - l for l in ls]
    ms = list(ls)
    span = 1
    while 2 * span < c:
        ms = [_dot(m, m, NN, MID) for m in ms]
        ps = _each(lambda p, m: p + _dot(p, m, NN, MID), ps, ms)
        span *= 2
    return tuple(ps)


def _inv_unit_lower_fwd(ls):
    ts = _inv_unit_lower(ls)
    return ts, ts


def _inv_unit_lower_bwd(ts, dts):
    inner = _each(lambda t, dt: _dot(t, dt, TN, MID), ts, dts)
    return (tuple(_each(lambda x, t: -_dot(x, t, NT, MID), inner, ts)),)


_inv_unit_lower.defvjp(_inv_unit_lower_fwd, _inv_unit_lower_bwd)


def _gdn_chunk(states, qs, ks, vs, gfull, bfull):
    c = GDN_CHUNK
    heads = list(range(len(states)))
    lane = lax.broadcasted_iota(jnp.int32, (c, LANES), 1)
    row = lax.broadcasted_iota(jnp.int32, (c, LANES), 0)
    r = lax.broadcasted_iota(jnp.int32, (c, c), 0)
    cc = lax.broadcasted_iota(jnp.int32, (c, c), 1)
    tri = (r >= cc).astype(f32)
    first_lane = (lane == 0).astype(f32)
    gs = [jnp.sum(jnp.where(lane == GDN_HEADS + h, gfull, 0.0), axis=1, keepdims=True) for h in heads]
    betas = [jnp.sum(jnp.where(lane == h, bfull, 0.0), axis=1, keepdims=True) for h in heads]
    gcs = [_dot(tri, jnp.broadcast_to(g, (c, LANES)), NN, HI) for g in gs]
    gc_cols = [_dot(first_lane, gc, NT, HI) for gc in gcs]
    gc_rows = [jnp.sum(jnp.where(lane == 0, gc, 0.0), axis=1, keepdims=True) for gc in gcs]
    decays = _each(lambda gr, gcl: jnp.exp(jnp.where(r >= cc, gr - gcl, NEG_BIG)), gc_rows, gc_cols)
    qs = [q * GDN_DK ** -0.5 for q in qs]
    k_betas = _each(lambda k, b: k * b, ks, betas)
    v_betas = _each(lambda v, b: v * b, vs, betas)
    egcs = [jnp.exp(gc) for gc in gcs]
    kkts = _each(lambda kb, k, d: _bdot(kb, k, NT) * d, k_betas, ks, decays)
    ts = _inv_unit_lower(tuple(jnp.where(r > cc, kkt, 0.0) for kkt in kkts))
    us = _each(lambda t, vb: _dot(t, vb, NN, MID), ts, v_betas)
    ws = _each(lambda t, kb, e: _dot(t, kb * e, NN, MID), ts, k_betas, egcs)
    qks = _each(lambda q, k, d: jnp.where(r >= cc, _bdot(q, k, NT) * d, 0.0), qs, ks, decays)
    gls = [jnp.sum(jnp.where(row == c - 1, gc, 0.0), axis=0, keepdims=True) for gc in gcs]
    k_tails = _each(lambda k, gl, gc: k * jnp.exp(gl - gc), ks, gls, gcs)
    v_news = _each(lambda u, w, st: u - _bdot(w, st, NN), us, ws, states)
    os_ = _each(lambda q, e, st, qk, vn: _bdot(q * e, st, NN) + _bdot(qk, vn, NN), qs, egcs, states, qks, v_news)
    new_states = _each(lambda st, gl, kt, vn: st * jnp.exp(gl) + _bdot(kt, vn, TN), states, gls, k_tails, v_news)
    return tuple(os_), tuple(new_states)


def _head_cols(h):
    return slice(h * LANES, (h + 1) * LANES)


def _gdn_fwd_call(q, k, v, gfull, bfull):
    s = q.shape[0]
    c = GDN_CHUNK
    n = s // c

    def body(q_ref, k_ref, v_ref, g_ref, b_ref, o_ref, st_ref, state):
        @pl.when(pl.program_id(0) == 0)
        def _():
            state[...] = jnp.zeros_like(state)

        heads = range(GDN_HEADS)
        states = [state[h] for h in heads]
        outs, new_states = _gdn_chunk(states, *[[ref[:, _head_cols(h)] for h in heads] for ref in (q_ref, k_ref, v_ref)],
                                      g_ref[...], b_ref[...])
        for h in heads:
            st_ref[0, h] = states[h]
            o_ref[:, _head_cols(h)] = outs[h]
            state[h] = new_states[h]

    hd = pl.BlockSpec((c, GDN_HEADS * LANES), lambda ni: (ni, 0))
    sh = pl.BlockSpec((c, LANES), lambda ni: (ni, 0))
    return pl.pallas_call(
        body, name="gdn_fwd", grid=(n,), in_specs=[hd, hd, hd, sh, sh],
        out_specs=[hd, pl.BlockSpec((1, GDN_HEADS, GDN_DK, GDN_DV), lambda ni: (ni, 0, 0, 0))],
        out_shape=[jax.ShapeDtypeStruct((s, GDN_HEADS * GDN_DV), f32),
                   jax.ShapeDtypeStruct((n, GDN_HEADS, GDN_DK, GDN_DV), f32)],
        scratch_shapes=[pltpu.VMEM((GDN_HEADS, GDN_DK, GDN_DV), f32)],
        compiler_params=_params(("arbitrary",)))(q, k, v, gfull, bfull)


def _gdn_bwd_call(q, k, v, gfull, bfull, states, do):
    s = q.shape[0]
    c = GDN_CHUNK
    n = s // c

    def body(q_ref, k_ref, v_ref, g_ref, b_ref, st_ref, do_ref, dq_ref, dk_ref, dv_ref, dg_ref, db_ref, dstate):
        @pl.when(pl.program_id(0) == 0)
        def _():
            dstate[...] = jnp.zeros_like(dstate)

        heads = range(GDN_HEADS)
        per_head = [tuple(ref[:, _head_cols(h)] for h in heads) for ref in (q_ref, k_ref, v_ref)]
        _, vjp = jax.vjp(_gdn_chunk, tuple(st_ref[0, h] for h in heads), *per_head, g_ref[...], b_ref[...])
        ds0, dq, dk, dv, dg, db = vjp((tuple(do_ref[:, _head_cols(h)] for h in heads), tuple(dstate[h] for h in heads)))
        for h in heads:
            dstate[h] = ds0[h]
            dq_ref[:, _head_cols(h)] = dq[h]
            dk_ref[:, _head_cols(h)] = dk[h]
            dv_ref[:, _head_cols(h)] = dv[h]
        dg_ref[...] = dg
        db_ref[...] = db

    hd = pl.BlockSpec((c, GDN_HEADS * LANES), lambda st: (n - 1 - st, 0))
    sh = pl.BlockSpec((c, LANES), lambda st: (n - 1 - st, 0))
    big = jax.ShapeDtypeStruct((s, GDN_HEADS * GDN_DV), f32)
    small = jax.ShapeDtypeStruct((s, LANES), f32)
    return pl.pallas_call(
        body, name="gdn_bwd", grid=(n,),
        in_specs=[hd, hd, hd, sh, sh,
                  pl.BlockSpec((1, GDN_HEADS, GDN_DK, GDN_DV), lambda st: (n - 1 - st, 0, 0, 0)), hd],
        out_specs=[hd, hd, hd, sh, sh], out_shape=[big, big, big, small, small],
        scratch_shapes=[pltpu.VMEM((GDN_HEADS, GDN_DK, GDN_DV), f32)],
        compiler_params=_params(("arbitrary",)))(q, k, v, gfull, bfull, states, do)


@jax.custom_vjp
def gdn_core(q, k, v, gfull, bfull):
    return _gdn_fwd_call(q, k, v, gfull, bfull)[0]


def _gdn_core_fwd(q, k, v, gfull, bfull):
    o, states = _gdn_fwd_call(q, k, v, gfull, bfull)
    return o, (q, k, v, gfull, bfull, states)


def _gdn_core_bwd(res, do):
    return tuple(_gdn_bwd_call(*res, do))


gdn_core.defvjp(_gdn_core_fwd, _gdn_core_bwd)


def _l2norm(t):
    return t * lax.rsqrt(jnp.sum(t * t, axis=-1, keepdims=True) + EPS)


def _gdn_pre(qc, kc, vc):
    return _l2norm(jax.nn.silu(qc)), _l2norm(jax.nn.silu(kc)), jax.nn.silu(vc)


def _gdn_gates(gba, a_log, dt_bias):
    beta = jax.nn.sigmoid(gba)
    g = -jnp.exp(a_log) * jax.nn.softplus(gba + dt_bias)
    return beta, g


def _gdn_post(o, z, w):
    return (_rms(o, w) * jax.nn.silu(z),)


def _merge(u0, u1, u2, g0, g1, g2, b0, b1, b2):
    return (jax.nn.sigmoid(g0 + b0) * u0 + jax.nn.sigmoid(g1 + b1) * u1 + jax.nn.sigmoid(g2 + b2) * u2,)


def _rope_q(qa, qb, ctab, stab):
    return (qa * ctab + qb * stab,)


def _rope_k(kp, kra, krb, ctab, stab):
    return (kp + kra * ctab + krb * stab,)


def _swap_rot(w):
    half = QK_ROPE // 2
    return jnp.concatenate([-w[..., half:], w[..., :half]], axis=-1)


def _ext_w_in(w):
    d = w.shape[0]
    z = lambda n: jnp.zeros((d, n), w.dtype)
    kr = w[:, 1024:1088]
    return jnp.concatenate([
        w[:, :1024], z(QK_NOPE), kr, z(64), z(QK_NOPE), _swap_rot(kr), z(64),
        w[:, 1088:7232], w[:, 7232:7248], z(112), w[:, 7248:], z(128)], axis=1)


def _ext_w_q(w):
    d = w.shape[0]
    w3 = w.reshape(d, MLA_HEADS, QK_NOPE + QK_ROPE)
    nope, pe = w3[..., :QK_NOPE], w3[..., QK_NOPE:]
    z64 = jnp.zeros((d, MLA_HEADS, 64), w.dtype)
    z128 = jnp.zeros((d, MLA_HEADS, QK_NOPE), w.dtype)
    a = jnp.concatenate([nope, pe, z64], axis=-1).reshape(d, MLA_HEADS * QK_PAD)
    b = jnp.concatenate([z128, _swap_rot(pe), z64], axis=-1).reshape(d, MLA_HEADS * QK_PAD)
    return jnp.concatenate([a, b], axis=1)


def _ext_w_kv(w):
    d = w.shape[0]
    w3 = w.reshape(d, MLA_HEADS, QK_NOPE + V_HEAD)
    kn, v = w3[..., :QK_NOPE], w3[..., QK_NOPE:]
    kp = jnp.concatenate([kn, jnp.zeros_like(kn)], axis=-1).reshape(d, MLA_HEADS * QK_PAD)
    return jnp.concatenate([kp, v.reshape(d, MLA_HEADS * V_HEAD)], axis=1)


def _lane_pad(vec8):
    return jnp.concatenate([jnp.zeros((8,), f32), vec8, jnp.zeros((LANES - 16,), f32)]).reshape(1, LANES)


def layer(x, p, ctab, stab, li):
    tag = f"l{li}_"
    h = rms_op(x, p["norm_mix"], tag + "norm_mix")
    proj = linear(h, _ext_w_in(p["w_in"]), tag + "w_in")
    (c_q, c_kv, kra, krb, lru_x, lru_y, g_q, g_k, g_v, g_z, g_ba, gl0, gl1, gl2, _) = split_cols(
        proj, [w for _, w in IN_PIECES])

    cqn = rms_op(c_q, p["mla_q_norm"], tag + "q_norm")
    ckvn = rms_op(c_kv, p["mla_kv_norm"], tag + "kv_norm")
    qall = linear(cqn, _ext_w_q(p["mla_w_uq"]), tag + "w_uq")
    kvall = linear(ckvn, _ext_w_kv(p["mla_w_ukv"]), tag + "w_ukv")
    qa, qb = split_cols(qall, [MLA_HEADS * QK_PAD] * 2)
    kp, v = split_cols(kvall, [MLA_HEADS * QK_PAD, MLA_HEADS * V_HEAD])
    q = row_op(tag + "rope_q", _rope_q, [(qa, "g"), (qb, "g"), (ctab, "sn"), (stab, "sn")], [], [(QK_PAD, f32)],
               MLA_HEADS)[0]
    k = row_op(tag + "rope_k", _rope_k, [(kp, "g"), (kra, "s"), (krb, "s"), (ctab, "sn"), (stab, "sn")], [],
               [(QK_PAD, f32)], MLA_HEADS)[0]
    y_mla = attention(q, k, v)

    xc = causal_conv(lru_x, p["lru_conv_w"], p["lru_conv_b"].reshape(1, -1), tag + "lru_conv")
    gshape = (LRU_BLOCKS, 1, LRU_BLOCK_W)
    a, bx = row_op(tag + "lru_gates", _lru_gates, [(xc, "g")],
                   [(p["lru_w_a"], "p"), (p["lru_b_a"].reshape(gshape), "p"), (p["lru_w_x"], "p"),
                    (p["lru_b_x"].reshape(gshape), "p"), (p["lru_lambda"].reshape(gshape), "p")],
                   [(LRU_BLOCK_W, f32)] * 2, LRU_BLOCKS)
    hs = lru_scan(a, bx)
    y_lru = row_op(tag + "lru_out", lambda hh, yy: (hh * jax.nn.gelu(yy),), [(hs, "g"), (lru_y, "g")], [],
                   [(LRU_WIDTH, f32)])[0]

    cw = p["gdn_conv_w"]
    nob = jnp.zeros((1, GDN_HEADS * GDN_DK), f32)
    qc = causal_conv(g_q, cw[:, :1024], nob, tag + "gdn_conv_q")
    kc = causal_conv(g_k, cw[:, 1024:2048], nob, tag + "gdn_conv_k")
    vc = causal_conv(g_v, cw[:, 2048:], nob, tag + "gdn_conv_v")
    qn, kn, vs = row_op(tag + "gdn_pre", _gdn_pre, [(qc, "g"), (kc, "g"), (vc, "g")], [], [(GDN_DK, f32)] * 3,
                        GDN_HEADS)
    bfull, gfull = row_op(tag + "gdn_gates", _gdn_gates, [(g_ba, "g")],
                          [(_lane_pad(p["gdn_a_log"]), "c"), (_lane_pad(p["gdn_dt_bias"]), "c")], [(LANES, f32)] * 2)
    o = gdn_core(qn, kn, vs, gfull, bfull)
    y_gdn = row_op(tag + "gdn_post", _gdn_post, [(o, "g"), (g_z, "g")], [(p["gdn_norm"].reshape(1, -1), "c")],
                   [(GDN_DV, f32)], GDN_HEADS)[0]

    wb = p["w_branch"]
    u0 = linear(y_mla, wb[0], tag + "w_branch0")
    u1 = linear(y_lru, wb[1], tag + "w_branch1")
    u2 = linear(y_gdn, wb[2], tag + "w_branch2")
    mixed = row_op(tag + "merge", _merge, [(u0, "g"), (u1, "g"), (u2, "g"), (gl0, "g"), (gl1, "g"), (gl2, "g")],
                   [(p["b_gate"][nb:nb + 1], "c") for nb in range(N_BRANCH)], [(D_MODEL, f32)])[0]
    x = x + linear(mixed, p["w_out"], tag + "w_out")

    h2 = rms_op(x, p["norm_ffn"], tag + "norm_ffn")
    gt = linear(h2, p["ffn_w_gate"], tag + "ffn_gate")
    up = linear(h2, p["ffn_w_up"], tag + "ffn_up")
    act = row_op(tag + "swiglu", lambda a_, b_: (jax.nn.silu(a_) * b_,), [(gt, "g"), (up, "g")], [],
                 [(FFN_HIDDEN, f32)])[0]
    return x + linear(act, p["ffn_w_down"], tag + "ffn_down")


def trunk(layers, x, ctab, stab):
    for li, p in enumerate(layers):
        x = layer(x, p, ctab, stab, li)
    return x


def loss_head(x, target, w):
    s, d = x.shape
    tile = _row_tile(s, [d] * 3)

    def fn(xv, wv, tv):
        err = jnp.square(_rms(xv, wv) - tv)
        return 0.5 * jnp.sum(jnp.mean(err, axis=-1, keepdims=True), axis=0, keepdims=True)

    def body(x_ref, t_ref, w_ref, loss_ref, dx_ref, dw_ref):
        @pl.when(pl.program_id(0) == 0)
        def _():
            loss_ref[...] = jnp.zeros_like(loss_ref)
            dw_ref[...] = jnp.zeros_like(dw_ref)

        tv = t_ref[...]
        val, vjp = jax.vjp(lambda xv, wv: fn(xv, wv, tv), x_ref[...], w_ref[...])
        dx, dw = vjp(jnp.ones((1, 1), f32))
        loss_ref[...] += jnp.broadcast_to(val, loss_ref.shape)
        dx_ref[...] = dx
        dw_ref[...] += dw

    loss, dx, dw = pl.pallas_call(
        body, name="loss_head", grid=(s // tile,),
        in_specs=[pl.BlockSpec((tile, d), lambda i: (i, 0)), pl.BlockSpec((tile, d), lambda i: (i, 0)),
                  pl.BlockSpec((1, d), lambda i: (0, 0))],
        out_specs=[pl.BlockSpec((SUBLANES, LANES), lambda i: (0, 0)), pl.BlockSpec((tile, d), lambda i: (i, 0)),
                   pl.BlockSpec((1, d), lambda i: (0, 0))],
        out_shape=[jax.ShapeDtypeStruct((SUBLANES, LANES), f32), jax.ShapeDtypeStruct((s, d), f32),
                   jax.ShapeDtypeStruct((1, d), f32)],
        compiler_params=_params(("arbitrary",)))(x, target, w.reshape(1, d))
    return loss[0, 0], dx, dw.reshape(d)


def local_step(layers, norm_final, x, positions, target):
    ctab, stab = rope_tables(positions.reshape(-1, 1))
    y, pull = jax.vjp(lambda ls, xx: trunk(ls, xx, ctab, stab), layers, x)
    loss, dy, d_final = loss_head(y, target, norm_final)
    d_layers, dx = pull(dy)
    return loss, dx, d_layers, d_final


def _flat2d(a):
    return a.reshape(-1, a.shape[-1])


def _ew_tile(rows, cols, n_arrays):
    budget = 16 * 1024 * 1024
    cap = max(SUBLANES, budget // (2 * 4 * cols * n_arrays))
    if rows <= cap:
        return rows
    return _largest_tile(rows, cap, SUBLANES)


def elementwise(name, fn, arrays, out_dtypes):
    shape = arrays[0].shape
    flat = [_flat2d(a) for a in arrays]
    rows, cols = flat[0].shape
    tile = _ew_tile(rows, cols, len(arrays) + len(out_dtypes))
    n_in = len(arrays)

    def body(*refs):
        res = fn(*[r[...] for r in refs[:n_in]])
        for o_ref, o in zip(refs[n_in:], res):
            o_ref[...] = o.astype(o_ref.dtype)

    spec = pl.BlockSpec((tile, cols), lambda i: (i, 0))
    res = pl.pallas_call(
        body, name=name, grid=(rows // tile,), in_specs=[spec] * n_in, out_specs=[spec] * len(out_dtypes),
        out_shape=[jax.ShapeDtypeStruct((rows, cols), dt) for dt in out_dtypes],
        compiler_params=_params(("parallel",)))(*flat)
    return [r.reshape(shape) for r in res]


def _adamw(w, g, m, v):
    m = ADAM_B1 * m + (1.0 - ADAM_B1) * g
    v = ADAM_B2 * v + (1.0 - ADAM_B2) * jnp.square(g)
    m_hat = m / (1.0 - ADAM_B1 ** ADAM_STEP)
    v_hat = v / (1.0 - ADAM_B2 ** ADAM_STEP)
    delta = -ADAM_LR * (m_hat / (jnp.sqrt(v_hat) + ADAM_EPS) + ADAM_WD * w)
    return delta, m, v


def _place():
    x, y, c = lax.axis_index("x"), lax.axis_index("y"), lax.axis_index("c")
    return x, y, c, 2 * x + y


CHIP_FLIPS = ((1, 0), (0, 1), (1, 1))


def _hbm_specs(n):
    return [pl.BlockSpec(memory_space=pl.ANY)] * n


def gather_weights(shards, name):
    n = len(shards)

    def body(*refs):
        src = refs[:n]
        out = refs[n:2 * n]
        send1, recv1, send2, recv2 = refs[2 * n:]
        x, y, c, j = _place()
        mine = pl.ds(c * HALF_LAYERS, HALF_LAYERS)
        firsts = []
        for a in range(n):
            for kk, (fx, fy) in enumerate(CHIP_FLIPS):
                cp = pltpu.make_async_remote_copy(
                    src_ref=src[a].at[mine], dst_ref=out[a].at[c, j], send_sem=send1.at[3 * a + kk],
                    recv_sem=recv1.at[3 * a + kk], device_id=(x ^ fx, y ^ fy, c), device_id_type=MESH)
                cp.start()
                firsts.append(cp)
        passed = []
        for a in range(n):
            for kk, (fx, fy) in enumerate(CHIP_FLIPS):
                jp = j ^ (2 * fx + fy)
                landed = out[a].at[c, jp]
                pltpu.make_async_remote_copy(
                    src_ref=landed, dst_ref=landed, send_sem=send1.at[3 * a + kk], recv_sem=recv1.at[3 * a + kk],
                    device_id=(x ^ fx, y ^ fy, c), device_id_type=MESH).wait_recv()
                cp = pltpu.make_async_remote_copy(
                    src_ref=landed, dst_ref=landed, send_sem=send2.at[3 * a + kk], recv_sem=recv2.at[3 * a + kk],
                    device_id=(x, y, 1 - c), device_id_type=MESH)
                cp.start()
                passed.append(cp)
        for a in range(n):
            for kk, (fx, fy) in enumerate(CHIP_FLIPS):
                jp = j ^ (2 * fx + fy)
                theirs = out[a].at[1 - c, jp]
                pltpu.make_async_remote_copy(
                    src_ref=theirs, dst_ref=theirs, send_sem=send2.at[3 * a + kk], recv_sem=recv2.at[3 * a + kk],
                    device_id=(x, y, 1 - c), device_id_type=MESH).wait_recv()
        for cp in firsts + passed:
            cp.wait_send()

    out_shape = [jax.ShapeDtypeStruct((2, N_CHIPS, HALF_LAYERS) + s.shape[1:], s.dtype) for s in shards]
    landed = pl.pallas_call(
        body, name=name, in_specs=_hbm_specs(n), out_specs=_hbm_specs(n), out_shape=out_shape,
        scratch_shapes=[pltpu.SemaphoreType.DMA((3 * n,)), pltpu.SemaphoreType.DMA((3 * n,)),
                        pltpu.SemaphoreType.DMA((3 * n,)), pltpu.SemaphoreType.DMA((3 * n,))])(*shards)
    chip = 2 * lax.axis_index("x") + lax.axis_index("y")
    filled = []
    for buf, s in zip(landed, shards):
        own = s.reshape((2, 1, HALF_LAYERS) + s.shape[1:])
        filled.append(lax.dynamic_update_slice(buf, own, (0, chip) + (0,) * (buf.ndim - 2)))
    return filled


def swap_with_sibling(arrays, name, lead_other_half=False):
    n = len(arrays)

    def body(*refs):
        src = refs[:n]
        out = refs[n:2 * n]
        send, recv = refs[2 * n:]
        x, y, c, _ = _place()
        cps = []
        for a in range(n):
            s_ref = src[a].at[:, pl.ds((1 - c) * HALF_LAYERS, HALF_LAYERS)] if lead_other_half else src[a]
            cp = pltpu.make_async_remote_copy(src_ref=s_ref, dst_ref=out[a], send_sem=send.at[a], recv_sem=recv.at[a],
                                              device_id=(x, y, 1 - c), device_id_type=MESH)
            cp.start()
            cps.append(cp)
        for cp in cps:
            cp.wait()

    if lead_other_half:
        out_shape = [jax.ShapeDtypeStruct((N_CHIPS, HALF_LAYERS) + a.shape[2:], a.dtype) for a in arrays]
    else:
        out_shape = [jax.ShapeDtypeStruct(a.shape, a.dtype) for a in arrays]
    return pl.pallas_call(
        body, name=name, in_specs=_hbm_specs(n), out_specs=_hbm_specs(n), out_shape=out_shape,
        scratch_shapes=[pltpu.SemaphoreType.DMA((n,)), pltpu.SemaphoreType.DMA((n,))])(*arrays)


def swap_rows(arrays, name, axis):
    n = len(arrays)
    count = 2 if axis == "x" else 1

    def body(*refs):
        src = refs[:n]
        out = refs[n:2 * n]
        send, recv = refs[2 * n:]
        x, y, c, _ = _place()
        peer = (1 - x, y, c) if axis == "x" else (x, 1 - y, c)
        start = 2 * (1 - x) if axis == "x" else 1 - y
        cps = []
        for a in range(n):
            cp = pltpu.make_async_remote_copy(src_ref=src[a].at[pl.ds(start, count)], dst_ref=out[a], send_sem=send.at[a],
                                              recv_sem=recv.at[a], device_id=peer, device_id_type=MESH)
            cp.start()
            cps.append(cp)
        for cp in cps:
            cp.wait()

    out_shape = [jax.ShapeDtypeStruct((count,) + a.shape[1:], a.dtype) for a in arrays]
    return pl.pallas_call(
        body, name=name, in_specs=_hbm_specs(n), out_specs=_hbm_specs(n), out_shape=out_shape,
        scratch_shapes=[pltpu.SemaphoreType.DMA((n,)), pltpu.SemaphoreType.DMA((n,))])(*arrays)


def reduce_to_shards(chip_major, names, group):
    x, y, c = lax.axis_index("x"), lax.axis_index("y"), lax.axis_index("c")
    theirs = swap_with_sibling(chip_major, "grad_pair_" + group, lead_other_half=True)
    pair_wire, pair32 = [], []
    for n, g, t in zip(names, chip_major, theirs):
        mine = lax.dynamic_slice_in_dim(g, c * HALF_LAYERS, HALF_LAYERS, axis=1)
        s_wire, s32 = elementwise("pair_sum_" + n, lambda a, b: (a.astype(f32) + b.astype(f32),) * 2, [mine, t],
                                  [mine.dtype, f32])
        pair_wire.append(s_wire)
        pair32.append(s32)
    from_x = swap_rows(pair_wire, "grad_x_" + group, "x")
    col_wire, col32 = [], []
    for n, s32, r in zip(names, pair32, from_x):
        own = lax.dynamic_slice_in_dim(s32, 2 * x, 2, axis=0)
        s_wire, s_32 = elementwise("col_sum_" + n, lambda a, b: (a + b.astype(f32),) * 2, [own, r], [r.dtype, f32])
        col_wire.append(s_wire)
        col32.append(s_32)
    from_y = swap_rows(col_wire, "grad_y_" + group, "y")
    halves = []
    for n, s32, r in zip(names, col32, from_y):
        own = lax.dynamic_index_in_dim(s32, y, axis=0, keepdims=False)
        halves.append(elementwise("chip_sum_" + n, lambda a, b: (a + b.astype(f32),), [own, r[0]], [f32])[0])
    theirs = swap_with_sibling(halves, "grad_share_" + group)
    return [jnp.where(c == 0, jnp.stack([mine, t]), jnp.stack([t, mine])) for mine, t in zip(halves, theirs)]


def gather_all_devices(vec, name):
    def body(src, out, send, recv, loc):
        x, y, c, _ = _place()
        me = 4 * x + 2 * y + c
        lc = pltpu.make_async_copy(src, out.at[me], loc.at[0])
        lc.start()
        cps = []
        for mask in range(1, 8):
            fx, fy, fc = (mask >> 2) & 1, (mask >> 1) & 1, mask & 1
            cp = pltpu.make_async_remote_copy(src_ref=src, dst_ref=out.at[me], send_sem=send.at[mask - 1],
                                              recv_sem=recv.at[mask - 1], device_id=(x ^ fx, y ^ fy, c ^ fc),
                                              device_id_type=MESH)
            cp.start()
            cps.append(cp)
        for mask in range(1, 8):
            cps[mask - 1].wait_send()
            theirs = out.at[me ^ mask]
            pltpu.make_async_remote_copy(src_ref=theirs, dst_ref=theirs, send_sem=send.at[mask - 1],
                                         recv_sem=recv.at[mask - 1], device_id=(x, y, c), device_id_type=MESH).wait_recv()
        lc.wait()

    return pl.pallas_call(
        body, name=name, in_specs=_hbm_specs(1), out_specs=pl.BlockSpec(memory_space=pl.ANY),
        out_shape=jax.ShapeDtypeStruct((8,) + vec.shape, vec.dtype),
        scratch_shapes=[pltpu.SemaphoreType.DMA((7,)), pltpu.SemaphoreType.DMA((7,)), pltpu.SemaphoreType.DMA((1,))])(vec)


def _to_chip_major(g, axis):
    shp = g.shape
    g = g.reshape(shp[:axis] + (N_CHIPS, shp[axis] // N_CHIPS) + shp[axis + 1:])
    return jnp.moveaxis(g, axis, 0)


def _from_chip_major(b, axis):
    b = jnp.moveaxis(b, 0, axis)
    shp = b.shape
    return b.reshape(shp[:axis] + (shp[axis] * shp[axis + 1],) + shp[axis + 2:])


def _pack(arrays):
    flat = jnp.concatenate([a.reshape(-1) for a in arrays])
    pad = (-flat.shape[0]) % (256 * LANES)
    return jnp.concatenate([flat, jnp.zeros((pad,), flat.dtype)]).reshape(-1, LANES)


def _unpack(packed, like):
    flat = packed.reshape(-1)
    out, off = [], 0
    for a in like:
        out.append(flat[off:off + a.size].reshape(a.shape))
        off += a.size
    return out


def kernel(x, positions, norm_mix, w_in, mla_q_norm, mla_w_uq, mla_kv_norm, mla_w_ukv, lru_conv_w, lru_conv_b, lru_w_a, lru_b_a, lru_w_x, lru_b_x, lru_lambda, gdn_conv_w, gdn_a_log, gdn_dt_bias, gdn_norm, w_branch, b_gate, w_out, norm_ffn, ffn_w_gate, ffn_w_up, ffn_w_down, norm_final, loss_target, m_norm_mix, m_w_in, m_mla_q_norm, m_mla_w_uq, m_mla_kv_norm, m_mla_w_ukv, m_lru_conv_w, m_lru_conv_b, m_lru_w_a, m_lru_b_a, m_lru_w_x, m_lru_b_x, m_lru_lambda, m_gdn_conv_w, m_gdn_a_log, m_gdn_dt_bias, m_gdn_norm, m_w_branch, m_b_gate, m_w_out, m_norm_ffn, m_ffn_w_gate, m_ffn_w_up, m_ffn_w_down, m_norm_final, v_norm_mix, v_w_in, v_mla_q_norm, v_mla_w_uq, v_mla_kv_norm, v_mla_w_ukv, v_lru_conv_w, v_lru_conv_b, v_lru_w_a, v_lru_b_a, v_lru_w_x, v_lru_b_x, v_lru_lambda, v_gdn_conv_w, v_gdn_a_log, v_gdn_dt_bias, v_gdn_norm, v_w_branch, v_b_gate, v_w_out, v_norm_ffn, v_ffn_w_gate, v_ffn_w_up, v_ffn_w_down, v_norm_final):
    given = dict(locals())
    w = {n: given[n] for n in WEIGHTS}
    m = {n: given["m_" + n] for n in WEIGHTS}
    v = {n: given["v_" + n] for n in WEIGHTS}

    wire = {n: (w[n] if n in WIRE_F32 else elementwise("cast_" + n, lambda a: (a,), [w[n]], [bf16])[0]) for n in SHARDED}
    big = [n for n in SHARDED if n not in WIRE_F32]
    gathered = dict(zip(big, gather_weights([wire[n] for n in big], "gather_big")))
    gathered.update(zip(WIRE_F32, gather_weights([wire[n] for n in WIRE_F32], "gather_small")))
    layers = []
    for li in range(DEPTH):
        p = {n: _from_chip_major(gathered[n][li // HALF_LAYERS, :, li % HALF_LAYERS], SHARD_AXIS[n]) for n in SHARDED}
        p.update({n: w[n][li] for n in REPLICATED if n != "norm_final"})
        layers.append(p)

    loss, grad_x, d_layers, d_final = local_step(layers, norm_final, x[0], positions[0], loss_target[0])
    loss = lax.psum(loss, ("x", "y", "c"))

    grads = {}
    chip_major = {n: jnp.stack([_to_chip_major(d_layers[li][n], SHARD_AXIS[n]) for li in range(DEPTH)], axis=1)
                  for n in SHARDED}
    for group, names in (("big", big), ("small", list(WIRE_F32))):
        for n, both in zip(names, reduce_to_shards([chip_major[n] for n in names], names, group)):
            grads[n] = both.reshape(w[n].shape)

    rep_g = [jnp.stack([d_layers[li][n] for li in range(DEPTH)]) for n in REPLICATED if n != "norm_final"] + [d_final]
    everyone = gather_all_devices(_pack(rep_g), "grad_replicated")
    total = elementwise("replicated_sum", lambda *a: (functools.reduce(lambda p, q: p + q, a),),
                        [everyone[d] for d in range(8)], [f32])[0]
    grads.update(zip(REPLICATED, _unpack(total, [w[n] for n in REPLICATED])))

    delta, new_m, new_v = {}, {}, {}
    for n in SHARDED:
        delta[n], new_m[n], new_v[n] = elementwise("adamw_" + n, _adamw, [w[n], grads[n], m[n], v[n]], [f32] * 3)
    rep = [n for n in REPLICATED]
    pd, pm, pv = elementwise("adamw_replicated", _adamw,
                             [_pack([w[n] for n in rep]), total, _pack([m[n] for n in rep]), _pack([v[n] for n in rep])],
                             [f32] * 3)
    for dst, packed in ((delta, pd), (new_m, pm), (new_v, pv)):
        dst.update(zip(rep, _unpack(packed, [w[n] for n in rep])))

    return (loss, grad_x[None], *[grads[n] for n in WEIGHTS], *[delta[n] for n in WEIGHTS],
            *[new_m[n] for n in WEIGHTS], *[new_v[n] for n in WEIGHTS])
```
